```python
import math
import jax, jax.numpy as jnp
from jax import lax
import numpy as np

D_MODEL = 4096
BATCH = 2
SEQ = 4096
DEPTH = 2

N_EVEN = (DEPTH + 1) // 2
N_ODD = DEPTH // 2
N_MOD = 6
ADA_INIT_SCALE = 0.5
NORM_EPS = 1e-6

CONV_CH = D_MODEL // 2
CONV_WIDTH = 31
DIFF_HEADS = 8
DIFF_HEAD_DIM = D_MODEL // 2 // (2 * DIFF_HEADS)
DIFF_V_DIM = 2 * DIFF_HEAD_DIM
DIFF_QK = 2 * DIFF_HEADS * DIFF_HEAD_DIM
DIFF_V = DIFF_HEADS * DIFF_V_DIM
DIFF_SUBLN_EPS = 1e-5
Q_BLOCK = 128
ROPE_THETA = 10000.0
EVEN_IN = 2 * CONV_CH + 2 * DIFF_QK + DIFF_V
EVEN_MIX = CONV_CH + DIFF_V

GLA_HEADS = 4
GLA_DK = D_MODEL // 2 // GLA_HEADS
GLA_DV = D_MODEL // GLA_HEADS
GLA_RANK = 16
GLA_TAU = 16.0
GLA_CHUNK = 64
GLA_K = GLA_HEADS * GLA_DK
GLA_V = GLA_HEADS * GLA_DV
ODD_IN = 2 * GLA_K + 2 * GLA_V + GLA_RANK

D_FF = ((8 * D_MODEL // 3 + 255) // 256) * 256
N_EXPERTS = 8
TOP_K = 2
D_FF_EXPERT = D_MODEL

kernel_name = "hybrid_conv_diffattn_gla_moe_adaln"


def rms_norm(x, g, eps=NORM_EPS):
    xf = x.astype(jnp.float32)
    y = xf * lax.rsqrt(jnp.mean(xf * xf, axis=-1, keepdims=True) + eps)
    return (y * g).astype(x.dtype)


def layer_norm(x, g, b, eps=NORM_EPS):
    xf = x.astype(jnp.float32)
    mu = jnp.mean(xf, axis=-1, keepdims=True)
    var = jnp.mean(jnp.square(xf - mu), axis=-1, keepdims=True)
    return ((xf - mu) * lax.rsqrt(var + eps) * g + b).astype(x.dtype)


def modulate(h, shift, scale):
    return h * (1.0 + scale[:, None, :]) + shift[:, None, :]


def rope(x, positions):
    d = x.shape[-1]
    inv = ROPE_THETA ** (-jnp.arange(0, d, 2, dtype=jnp.float32) / d)
    ang = positions.astype(jnp.float32)[:, None] * inv[None, :]
    cos, sin = jnp.cos(ang)[None, :, None, :], jnp.sin(ang)[None, :, None, :]
    xf = x.astype(jnp.float32)
    x1, x2 = xf[..., : d // 2], xf[..., d // 2:]
    return jnp.concatenate([x1 * cos - x2 * sin, x2 * cos + x1 * sin], axis=-1).astype(x.dtype)


def conformer_conv(a_val, a_gate, conv_w, conv_b, ln_g, ln_b):
    u = a_val * jax.nn.sigmoid(a_gate)
    u = lax.conv_general_dilated(
        u, conv_w[:, None, :].astype(u.dtype), window_strides=(1,),
        padding=[(CONV_WIDTH - 1, 0)], dimension_numbers=("NWC", "WIO", "NWC"),
        feature_group_count=CONV_CH) + conv_b
    return jax.nn.silu(layer_norm(u, ln_g, ln_b))


def diff_attention(q, k, v, lam, subln_g, lambda_init):
    bsz, h2, s, d = q.shape
    h = h2 // 2
    nb = s // Q_BLOCK
    qb = q.reshape(bsz, h2, nb, Q_BLOCK, d).transpose(2, 0, 1, 3, 4)
    kpos = jnp.arange(s)
    scale = d ** -0.5

    def block(args):
        qi, i = args
        sc = jnp.einsum("bhqd,bhkd->bhqk", qi, k).astype(jnp.float32) * scale
        qpos = i * Q_BLOCK + jnp.arange(Q_BLOCK)
        sc = jnp.where(kpos[None, :] <= qpos[:, None], sc, -jnp.inf)
        p = jax.nn.softmax(sc, axis=-1).reshape(bsz, h, 2, Q_BLOCK, s)
        a = p[:, :, 0] - lam * p[:, :, 1]
        return jnp.einsum("bhqk,bhkv->bhqv", a.astype(v.dtype), v)

    o = lax.map(block, (qb, jnp.arange(nb)))
    o = o.transpose(1, 2, 0, 3, 4).reshape(bsz, h, s, v.shape[-1])
    o = rms_norm(o, subln_g, DIFF_SUBLN_EPS) * (1.0 - lambda_init)
    return o.transpose(0, 2, 1, 3).reshape(bsz, s, h * v.shape[-1])


def gla_chunked(q, k, v, log_a):
    bsz, h, s, dk = q.shape
    dv = v.shape[-1]
    n = s // GLA_CHUNK
    f32 = jnp.float32
    q, k, log_a = [t.astype(f32).reshape(bsz, h, n, GLA_CHUNK, dk) for t in (q, k, log_a)]
    v = v.astype(f32).reshape(bsz, h, n, GLA_CHUNK, dv)
    b = jnp.cumsum(log_a, axis=3)
    b_last = b[:, :, :, -1:, :]
    q_t = q * jnp.exp(b)
    k_t = k * jnp.exp(-b)
    k_dec = k * jnp.exp(b_last - b)
    causal = jnp.tril(jnp.ones((GLA_CHUNK, GLA_CHUNK), dtype=bool))
    attn = jnp.where(causal, jnp.einsum("bhnid,bhnjd->bhnij", q_t, k_t), 0.0)
    o_intra = jnp.einsum("bhnij,bhnjv->bhniv", attn, v)

    def step(state, inp):
        qc, kc, vc, dc = inp
        o = jnp.einsum("bhid,bhdv->bhiv", qc, state)
        state = dc[..., None] * state + jnp.einsum("bhid,bhiv->bhdv", kc, vc)
        return state, o

    xs = (jnp.moveaxis(q_t, 2, 0), jnp.moveaxis(k_dec, 2, 0), jnp.moveaxis(v, 2, 0),
          jnp.moveaxis(jnp.exp(b_last[:, :, :, 0, :]), 2, 0))
    _, o_inter = lax.scan(step, jnp.zeros((bsz, h, dk, dv), f32), xs)
    o = o_intra + jnp.moveaxis(o_inter, 0, 2)
    return o.reshape(bsz, h, s, dv)


def swiglu(h, w_gate, w_up, w_down):
    return (jax.nn.silu(h @ w_gate) * (h @ w_up)) @ w_down


def moe_swiglu(h, w_router, w_gate, w_up, w_down):
    bsz, s, d = h.shape
    t = h.reshape(bsz * s, d)
    logits = (t @ w_router).astype(jnp.float32)
    top_v, top_i = lax.top_k(logits, TOP_K)
    weights = jax.nn.softmax(top_v, axis=-1)
    gates = jnp.sum(jax.nn.one_hot(top_i, N_EXPERTS, dtype=jnp.float32) * weights[..., None], axis=1)
    out = jnp.zeros((bsz * s, d), dtype=jnp.float32)
    for e in range(N_EXPERTS):
        out = out + gates[:, e, None] * swiglu(t, w_gate[e], w_up[e], w_down[e]).astype(jnp.float32)
    return out.astype(h.dtype).reshape(bsz, s, d)


def _normal(key, shape, scale):
    return jax.random.normal(key, shape, jnp.float32) * scale


def setup_inputs(seed: int = 0) -> dict:
    key = jax.random.key(seed)
    ks = jax.random.split(key, 32)
    D = D_MODEL
    return {
        "x": _normal(ks[0], (BATCH, SEQ, D), 1.0),
        "c": _normal(ks[1], (BATCH, D), 1.0),
        "norm_gains": 1.0 + _normal(ks[2], (DEPTH, 2, D), 0.02),
        "ada_w": _normal(ks[3], (DEPTH, D, N_MOD * D), ADA_INIT_SCALE * D ** -0.5),
        "ada_b": _normal(ks[4], (DEPTH, N_MOD * D), 0.02),
        "e_w_in": _normal(ks[5], (N_EVEN, D, EVEN_IN), D ** -0.5),
        "e_conv_w": _normal(ks[6], (N_EVEN, CONV_WIDTH, CONV_CH), CONV_WIDTH ** -0.5),
        "e_conv_b": _normal(ks[7], (N_EVEN, CONV_CH), 0.02),
        "e_conv_ln_g": 1.0 + _normal(ks[8], (N_EVEN, CONV_CH), 0.02),
        "e_conv_ln_b": _normal(ks[9], (N_EVEN, CONV_CH), 0.02),
        "e_diff_lambda": _normal(ks[10], (N_EVEN, 4, DIFF_HEAD_DIM), 0.1),
        "e_diff_subln": 1.0 + _normal(ks[11], (N_EVEN, DIFF_V_DIM), 0.02),
        "e_w_out": _normal(ks[12], (N_EVEN, EVEN_MIX, D), EVEN_MIX ** -0.5),
        "e_ffn_gate": _normal(ks[13], (N_EVEN, D, D_FF), D ** -0.5),
        "e_ffn_up": _normal(ks[14], (N_EVEN, D, D_FF), D ** -0.5),
        "e_ffn_down": _normal(ks[15], (N_EVEN, D_FF, D), D_FF ** -0.5),
        "o_w_in": _normal(ks[16], (N_ODD, D, ODD_IN), D ** -0.5),
        "o_gate_w2": _normal(ks[17], (N_ODD, GLA_RANK, GLA_K), GLA_RANK ** -0.5),
        "o_gate_b": _normal(ks[18], (N_ODD, GLA_K), 0.02),
        "o_gla_norm": 1.0 + _normal(ks[19], (N_ODD, GLA_DV), 0.02),
        "o_w_out": _normal(ks[20], (N_ODD, GLA_V, D), GLA_V ** -0.5),
        "o_router": _normal(ks[21], (N_ODD, D, N_EXPERTS), D ** -0.5),
        "o_exp_gate": _normal(ks[22], (N_ODD, N_EXPERTS, D, D_FF_EXPERT), D ** -0.5),
        "o_exp_up": _normal(ks[23], (N_ODD, N_EXPERTS, D, D_FF_EXPERT), D ** -0.5),
        "o_exp_down": _normal(ks[24], (N_ODD, N_EXPERTS, D_FF_EXPERT, D), D_FF_EXPERT ** -0.5),
        "final_norm": 1.0 + _normal(ks[25], (D,), 0.02),
    }


def reference(x, c, norm_gains, ada_w, ada_b, e_w_in, e_conv_w, e_conv_b, e_conv_ln_g,
              e_conv_ln_b, e_diff_lambda, e_diff_subln, e_w_out, e_ffn_gate, e_ffn_up,
              e_ffn_down, o_w_in, o_gate_w2, o_gate_b, o_gla_norm, o_w_out, o_router,
              o_exp_gate, o_exp_up, o_exp_down, final_norm):
    bsz, s, _ = x.shape
    positions = jnp.arange(s)
    c_act = jax.nn.silu(c)
    for layer in range(DEPTH):
        mod = c_act @ ada_w[layer] + ada_b[layer]
        sh_m, sc_m, g_m, sh_f, sc_f, g_f = jnp.split(mod, N_MOD, axis=-1)
        h = modulate(rms_norm(x, norm_gains[layer, 0]), sh_m, sc_m)
        if layer % 2 == 0:
            i = layer // 2
            proj = h @ e_w_in[i]
            a_val, a_gate, q, k, v = jnp.split(
                proj, [CONV_CH, 2 * CONV_CH, 2 * CONV_CH + DIFF_QK, 2 * CONV_CH + 2 * DIFF_QK], axis=-1)
            y_a = conformer_conv(a_val, a_gate, e_conv_w[i], e_conv_b[i], e_conv_ln_g[i], e_conv_ln_b[i])
            q = rope(q.reshape(bsz, s, 2 * DIFF_HEADS, DIFF_HEAD_DIM), positions).transpose(0, 2, 1, 3)
            k = rope(k.reshape(bsz, s, 2 * DIFF_HEADS, DIFF_HEAD_DIM), positions).transpose(0, 2, 1, 3)
            v = v.reshape(bsz, s, DIFF_HEADS, DIFF_V_DIM).transpose(0, 2, 1, 3)
            lam_p = e_diff_lambda[i].astype(jnp.float32)
            lambda_init = 0.8 - 0.6 * math.exp(-0.3 * layer)
            lam = (jnp.exp(jnp.sum(lam_p[0] * lam_p[1])) - jnp.exp(jnp.sum(lam_p[2] * lam_p[3]))
                   + lambda_init)
            y_b = diff_attention(q, k, v, lam, e_diff_subln[i], lambda_init)
            y = jnp.concatenate([y_a, y_b.astype(y_a.dtype)], axis=-1) @ e_w_out[i]
            x = x + g_m[:, None, :] * y
            h = modulate(rms_norm(x, norm_gains[layer, 1]), sh_f, sc_f)
            x = x + g_f[:, None, :] * swiglu(h, e_ffn_gate[i], e_ffn_up[i], e_ffn_down[i])
        else:
            i = layer // 2
            proj = h @ o_w_in[i]
            q, k, v, r, g1 = jnp.split(
                proj, [GLA_K, 2 * GLA_K, 2 * GLA_K + GLA_V, 2 * GLA_K + 2 * GLA_V], axis=-1)
            gpre = g1 @ o_gate_w2[i] + o_gate_b[i]
            log_a = jax.nn.log_sigmoid(gpre.astype(jnp.float32)) / GLA_TAU
            heads = lambda t, dh: t.reshape(bsz, s, GLA_HEADS, dh).transpose(0, 2, 1, 3)
            o = gla_chunked(heads(q * GLA_DK ** -0.5, GLA_DK), heads(k, GLA_DK),
                            heads(v, GLA_DV), heads(log_a, GLA_DK))
            o = rms_norm(o, o_gla_norm[i]).transpose(0, 2, 1, 3).reshape(bsz, s, GLA_V)
            y = (o.astype(x.dtype) * jax.nn.silu(r)) @ o_w_out[i]
            x = x + g_m[:, None, :] * y
            h = modulate(rms_norm(x, norm_gains[layer, 1]), sh_f, sc_f)
            x = x + g_f[:, None, :] * moe_swiglu(h, o_router[i], o_exp_gate[i], o_exp_up[i], o_exp_down[i])
    return rms_norm(x, final_norm)
```

```python
import functools
import math

import jax
import jax.numpy as jnp
from jax import lax
from jax.experimental import pallas as pl
from jax.experimental.pallas import tpu as pltpu

f32 = jnp.float32
bf16 = jnp.bfloat16

N_MOD = 6
NORM_EPS = 1e-6
CONV_WIDTH = 31
CONV_HALO = 32
DIFF_HEADS = 8
DIFF_SUBLN_EPS = 1e-5
ROPE_THETA = 10000.0
GLA_HEADS = 4
GLA_RANK = 16
GLA_TAU = 16.0
GLA_CHUNK = 64
N_EXPERTS = 8
TOP_K = 2
LANES = 128
VMEM_LIMIT_BYTES = 58 * 1024 * 1024


def _cparams(n_axes):
    return pltpu.CompilerParams(
        dimension_semantics=("arbitrary",) * n_axes,
        vmem_limit_bytes=VMEM_LIMIT_BYTES)


def _sigmoid(x):
    return 1.0 / (1.0 + jnp.exp(-x))


def _silu(x):
    return x * _sigmoid(x)


def _ada_kernel(c_ref, w_ref, b_ref, o_ref):
    ca = _silu(c_ref[...]).astype(bf16)
    w = w_ref[...].astype(bf16)
    o_ref[...] = jnp.dot(ca, w, preferred_element_type=f32) + b_ref[...]


def _ada(c, ada_w, ada_b):
    depth, d, n6 = ada_w.shape
    bsz = c.shape[0]
    rows = 8
    cp = jnp.zeros((rows, d), f32).at[:bsz].set(c)
    bn = min(512, n6)
    out = pl.pallas_call(
        _ada_kernel,
        grid=(depth, n6 // bn),
        in_specs=[
            pl.BlockSpec((rows, d), lambda l, j: (0, 0)),
            pl.BlockSpec((None, d, bn), lambda l, j: (l, 0, j)),
            pl.BlockSpec((None, 1, bn), lambda l, j: (l, 0, j)),
        ],
        out_specs=pl.BlockSpec((None, rows, bn), lambda l, j: (l, 0, j)),
        out_shape=jax.ShapeDtypeStruct((depth, rows, n6), f32),
        compiler_params=_cparams(2),
        name="ada_mod",
    )(cp, ada_w, ada_b.reshape(depth, 1, n6))
    return out[:, :bsz].reshape(depth, bsz, N_MOD, 1, d)


def _normmod_value(x_ref, g_ref, sh_ref, sc_ref):
    x = x_ref[...]
    y = x * lax.rsqrt(jnp.mean(x * x, axis=-1, keepdims=True) + NORM_EPS)
    y = y * g_ref[...]
    return y * (1.0 + sc_ref[...]) + sh_ref[...]


def _normmod_kernel(x_ref, g_ref, sh_ref, sc_ref, o_ref):
    o_ref[...] = _normmod_value(x_ref, g_ref, sh_ref, sc_ref).astype(o_ref.dtype)


def _normmod_specs(bm, d, seq, i_shift, i_scale):
    return [
        pl.BlockSpec((bm, d), lambda i: (i, 0)),
        pl.BlockSpec((1, d), lambda i: (0, 0)),
        pl.BlockSpec((None, None, 1, d), lambda i: (i * bm // seq, i_shift, 0, 0)),
        pl.BlockSpec((None, None, 1, d), lambda i: (i * bm // seq, i_scale, 0, 0)),
    ]


def _normmod(x, gain, mod, i_shift, i_scale, seq):
    n, d = x.shape
    bm = min(512, seq)
    return pl.pallas_call(
        _normmod_kernel,
        grid=(n // bm,),
        in_specs=_normmod_specs(bm, d, seq, i_shift, i_scale),
        out_specs=pl.BlockSpec((bm, d), lambda i: (i, 0)),
        out_shape=jax.ShapeDtypeStruct((n, d), bf16),
        compiler_params=_cparams(1),
        name="normmod",
    )(x, gain.reshape(1, d), mod, mod)


def _normmod_router_kernel(x_ref, g_ref, sh_ref, sc_ref, wr_ref, h_ref, idx_ref, wgt_ref):
    h = _normmod_value(x_ref, g_ref, sh_ref, sc_ref)
    h_ref[...] = h
    logits = jnp.dot(h, wr_ref[...], preferred_element_type=f32,
                     precision=lax.Precision.HIGHEST)
    col = lax.broadcasted_iota(jnp.int32, logits.shape, 1)
    neg = jnp.float32(-jnp.inf)
    l0 = jnp.where(col < N_EXPERTS, logits, neg)
    m1 = jnp.max(l0, axis=-1, keepdims=True)
    i1 = jnp.min(jnp.where(l0 == m1, col, LANES), axis=-1, keepdims=True)
    l1 = jnp.where(col == i1, neg, l0)
    m2 = jnp.max(l1, axis=-1, keepdims=True)
    i2 = jnp.min(jnp.where(l1 == m2, col, LANES), axis=-1, keepdims=True)
    e = jnp.exp(m2 - m1)
    w1 = 1.0 / (1.0 + e)
    w2 = e / (1.0 + e)
    idx_ref[...] = jnp.where(col == 0, i1, jnp.where(col == 1, i2, 0))
    wgt_ref[...] = jnp.where(col == 0, w1, jnp.where(col == 1, w2, 0.0))


def _normmod_router(x, gain, mod, i_shift, i_scale, seq, w_router):
    n, d = x.shape
    bm = min(256, seq)
    wr = jnp.zeros((d, LANES), f32).at[:, :N_EXPERTS].set(w_router)
    h, idx, wgt = pl.pallas_call(
        _normmod_router_kernel,
        grid=(n // bm,),
        in_specs=_normmod_specs(bm, d, seq, i_shift, i_scale)
        + [pl.BlockSpec((d, LANES), lambda i: (0, 0))],
        out_specs=[
            pl.BlockSpec((bm, d), lambda i: (i, 0)),
            pl.BlockSpec((bm, LANES), lambda i: (i, 0)),
            pl.BlockSpec((bm, LANES), lambda i: (i, 0)),
        ],
        out_shape=[
            jax.ShapeDtypeStruct((n, d), f32),
            jax.ShapeDtypeStruct((n, LANES), jnp.int32),
            jax.ShapeDtypeStruct((n, LANES), f32),
        ],
        compiler_params=_cparams(1),
        name="normmod_router",
    )(x, gain.reshape(1, d), mod, mod, wr)
    return h, idx[:, :TOP_K], wgt[:, :TOP_K]


def _gmm_kernel(te_ref, nt_ref, *refs, n_x, n_w, n_extra, k_parts, epilogue):
    x_refs = refs[:n_x]
    w_refs = refs[n_x:n_x + n_w]
    e_refs = refs[n_x + n_w:n_x + n_w + n_extra]
    o_ref = refs[n_x + n_w + n_extra]
    wb_ref = refs[n_x + n_w + n_extra + 1]
    i = pl.program_id(1)
    active = i < nt_ref[0]
    prev = te_ref[jnp.maximum(i - 1, 0)]
    changed = jnp.logical_and(active, jnp.logical_or(i == 0, te_ref[i] != prev))

    @pl.when(changed)
    def _():
        for t in range(n_w):
            wb_ref[t] = w_refs[t][...].astype(bf16)

    @pl.when(active)
    def _():
        accs = []
        for t in range(n_w):
            acc = None
            off = 0
            for p in range(n_x):
                kp = k_parts[p]
                part = jnp.dot(x_refs[p][...], wb_ref[t, off:off + kp, :],
                               preferred_element_type=f32)
                acc = part if acc is None else acc + part
                off += kp
            accs.append(acc)
        epilogue(accs, e_refs, o_ref)

    @pl.when(jnp.logical_not(active))
    def _():
        o_ref[...] = jnp.zeros(o_ref.shape, o_ref.dtype)


def _gmm(xs, ws, *, bm, bn, n_cols, k_block=None, k_index=0, epilogue, extras=(), extra_specs=(),
         out_dtype, tile_group=None, num_tiles=None, name):
    m = xs[0].shape[0]
    k_parts = tuple(x.shape[1] for x in xs) if k_block is None else (k_block,)
    k = sum(k_parts)
    n_tiles = m // bm
    if tile_group is None:
        tile_group = jnp.zeros((n_tiles,), jnp.int32)
        num_tiles = jnp.full((1,), n_tiles, jnp.int32)
    xk = k_index if k_block is not None else 0

    def row(i, nt):
        return jnp.minimum(i, nt[0] - 1)

    in_specs = [pl.BlockSpec((bm, kp), lambda j, i, te, nt: (row(i, nt), xk)) for kp in k_parts]
    in_specs += [pl.BlockSpec((None, k, bn), lambda j, i, te, nt: (te[row(i, nt)], xk, j))
                 for _ in ws]
    in_specs += list(extra_specs)
    kern = functools.partial(_gmm_kernel, n_x=len(xs), n_w=len(ws), n_extra=len(extras),
                             k_parts=k_parts, epilogue=epilogue)
    return pl.pallas_call(
        kern,
        grid_spec=pltpu.PrefetchScalarGridSpec(
            num_scalar_prefetch=2,
            grid=(n_cols // bn, n_tiles),
            in_specs=in_specs,
            out_specs=pl.BlockSpec((bm, bn), lambda j, i, te, nt: (i, j)),
            scratch_shapes=[pltpu.VMEM((len(ws), k, bn), bf16)],
        ),
        out_shape=jax.ShapeDtypeStruct((m, n_cols), out_dtype),
        compiler_params=_cparams(2),
        name=name,
    )(tile_group, num_tiles, *xs, *ws, *extras)


def _ep_plain(accs, e_refs, o_ref):
    o_ref[...] = accs[0].astype(o_ref.dtype)


def _ep_swiglu(accs, e_refs, o_ref):
    g, u = accs
    o_ref[...] = (_silu(g) * u).astype(o_ref.dtype)


def _ep_resid(accs, e_refs, o_ref):
    res_ref, gate_ref = e_refs
    o_ref[...] = res_ref[...] + gate_ref[...] * accs[0]


def _ep_resid_partial(accs, e_refs, o_ref):
    res_ref, gate_ref, part_ref = e_refs
    o_ref[...] = res_ref[...] + gate_ref[...] * (part_ref[...] + accs[0])


def _ep_rowscale(accs, e_refs, o_ref):
    (rs_ref,) = e_refs
    o_ref[...] = accs[0] * rs_ref[...]


def _ep_rope(accs, e_refs, o_ref, *, q_tiles, k_tiles, q_scale):
    cos_ref, sin_ref = e_refs
    acc = accs[0]
    j = pl.program_id(0)
    is_q = jnp.logical_and(j >= q_tiles[0], j < q_tiles[1])
    is_k = jnp.logical_and(j >= k_tiles[0], j < k_tiles[1])

    def roped(scale):
        cos = cos_ref[...]
        sin = sin_ref[...]
        outs = []
        for hh in range(acc.shape[1] // LANES):
            a = acc[:, hh * LANES:(hh + 1) * LANES]
            rot = pltpu.roll(a, LANES // 2, 1)
            outs.append((a * cos + rot * sin) * scale)
        return jnp.concatenate(outs, axis=1)

    @pl.when(is_q)
    def _():
        o_ref[...] = roped(q_scale).astype(o_ref.dtype)

    @pl.when(is_k)
    def _():
        o_ref[...] = roped(1.0).astype(o_ref.dtype)

    @pl.when(jnp.logical_not(jnp.logical_or(is_q, is_k)))
    def _():
        o_ref[...] = acc.astype(o_ref.dtype)


def _gate_spec(bm, bn, seq, i_gate):
    return pl.BlockSpec((None, None, 1, bn),
                        lambda j, i, te, nt: (jnp.minimum(i, nt[0] - 1) * bm // seq, i_gate, 0, j))


def _tile_spec(bm, bn):
    return pl.BlockSpec((bm, bn), lambda j, i, te, nt: (jnp.minimum(i, nt[0] - 1), j))


def _conv_kernel(val_ref, gate_ref, hval_ref, hgate_ref, w_ref, b_ref, lg_ref, lb_ref,
                 o_ref, u_scr, v_scr, *, ts, seq, rows):
    i = pl.program_id(0)
    first = (i * ts) % seq == 0
    hu = hval_ref[...].astype(f32) * _sigmoid(hgate_ref[...].astype(f32))
    u_scr[0:CONV_HALO, :] = jnp.where(first, 0.0, hu)
    u_scr[CONV_HALO:, :] = val_ref[...].astype(f32) * _sigmoid(gate_ref[...].astype(f32))
    lead = CONV_HALO - (CONV_WIDTH - 1)

    cw = 2 * LANES

    def chunk(c, carry):
        r0 = pl.multiple_of(c * rows, rows)
        for c0 in range(0, u_scr.shape[1], cw):
            win = u_scr[pl.ds(r0, rows + CONV_HALO), c0:c0 + cw]
            acc = jnp.zeros((rows, cw), f32)
            for j in range(CONV_WIDTH):
                acc = acc + w_ref[j:j + 1, c0:c0 + cw] * win[lead + j:lead + j + rows, :]
            v_scr[pl.ds(r0, rows), c0:c0 + cw] = acc
        return carry

    lax.fori_loop(0, ts // rows, chunk, 0)
    u = v_scr[...] + b_ref[...]
    mu = jnp.mean(u, axis=-1, keepdims=True)
    var = jnp.mean(jnp.square(u - mu), axis=-1, keepdims=True)
    y = (u - mu) * lax.rsqrt(var + NORM_EPS) * lg_ref[...] + lb_ref[...]
    o_ref[...] = _silu(y).astype(o_ref.dtype)


def _conv(proj, conv_w, conv_b, ln_g, ln_b, seq):
    n = proj.shape[0]
    ch = conv_w.shape[1]
    ts = min(256, seq)
    hb = ts // CONV_HALO
    halo_row = lambda i: jnp.maximum(i * hb - 1, 0)
    vec = lambda: pl.BlockSpec((1, ch), lambda i: (0, 0))
    kern = functools.partial(_conv_kernel, ts=ts, seq=seq, rows=32)
    return pl.pallas_call(
        kern,
        grid=(n // ts,),
        in_specs=[
            pl.BlockSpec((ts, ch), lambda i: (i, 0)),
            pl.BlockSpec((ts, ch), lambda i: (i, 1)),
            pl.BlockSpec((CONV_HALO, ch), lambda i: (halo_row(i), 0)),
            pl.BlockSpec((CONV_HALO, ch), lambda i: (halo_row(i), 1)),
            pl.BlockSpec((CONV_WIDTH, ch), lambda i: (0, 0)),
            vec(), vec(), vec(),
        ],
        out_specs=pl.BlockSpec((ts, ch), lambda i: (i, 0)),
        out_shape=jax.ShapeDtypeStruct((n, ch), bf16),
        scratch_shapes=[pltpu.VMEM((ts + CONV_HALO, ch), f32), pltpu.VMEM((ts, ch), f32)],
        compiler_params=_cparams(1),
        name="conformer_conv",
    )(proj, proj, proj, proj, conv_w, conv_b.reshape(1, ch), ln_g.reshape(1, ch), ln_b.reshape(1, ch))


def _attn_kernel(q_ref, k_ref, v_ref, lam_ref, g_ref, o_ref, m_scr, l_scr, acc_scr,
                 *, lambda_init, hd):
    qi = pl.program_id(2)
    ki = pl.program_id(3)

    @pl.when(ki == 0)
    def _():
        m_scr[...] = jnp.full(m_scr.shape, -jnp.inf, f32)
        l_scr[...] = jnp.zeros(l_scr.shape, f32)
        acc_scr[...] = jnp.zeros(acc_scr.shape, f32)

    def step(masked):
        v = v_ref[...]
        for s in range(2):
            q = q_ref[:, s * hd:(s + 1) * hd]
            k = k_ref[:, s * hd:(s + 1) * hd]
            sc = lax.dot_general(q, k, (((1,), (1,)), ((), ())), preferred_element_type=f32)
            if masked:
                row = lax.broadcasted_iota(jnp.int32, sc.shape, 0)
                col = lax.broadcasted_iota(jnp.int32, sc.shape, 1)
                sc = jnp.where(col <= row, sc, -jnp.inf)
            m_prev = m_scr[s]
            m_new = jnp.maximum(m_prev, jnp.max(sc, axis=-1, keepdims=True))
            alpha = jnp.exp(m_prev - m_new)
            p = jnp.exp(sc - m_new)
            l_scr[s] = alpha * l_scr[s] + jnp.sum(p, axis=-1, keepdims=True)
            acc_scr[s] = alpha * acc_scr[s] + jnp.dot(p.astype(bf16), v, preferred_element_type=f32)
            m_scr[s] = m_new

    @pl.when(ki < qi)
    def _():
        step(False)

    @pl.when(ki == qi)
    def _():
        step(True)
        lp = lam_ref[...]
        lam = (jnp.exp(jnp.sum(lp[0:1] * lp[1:2], axis=-1, keepdims=True))
               - jnp.exp(jnp.sum(lp[2:3] * lp[3:4], axis=-1, keepdims=True)) + lambda_init)
        o = acc_scr[0] / l_scr[0] - lam * (acc_scr[1] / l_scr[1])
        o = o * lax.rsqrt(jnp.mean(o * o, axis=-1, keepdims=True) + DIFF_SUBLN_EPS)
        o_ref[...] = ((o * g_ref[...]) * (1.0 - lambda_init)).astype(o_ref.dtype)


def _diff_attention(proj, lam_p, subln_g, bsz, seq, q_col0, k_col0, v_col0, lambda_init):
    n = proj.shape[0]
    hd = lam_p.shape[1]
    vd = 2 * hd
    tq = min(512, seq)
    nq = seq // tq
    kern = functools.partial(_attn_kernel, lambda_init=lambda_init, hd=hd)
    kv_row = lambda b, h, qi, ki: b * nq + jnp.minimum(ki, qi)
    return pl.pallas_call(
        kern,
        grid=(bsz, DIFF_HEADS, nq, nq),
        in_specs=[
            pl.BlockSpec((tq, vd), lambda b, h, qi, ki: (b * nq + qi, q_col0 // vd + h)),
            pl.BlockSpec((tq, vd), lambda b, h, qi, ki: (kv_row(b, h, qi, ki), k_col0 // vd + h)),
            pl.BlockSpec((tq, vd), lambda b, h, qi, ki: (kv_row(b, h, qi, ki), v_col0 // vd + h)),
            pl.BlockSpec((4, hd), lambda b, h, qi, ki: (0, 0)),
            pl.BlockSpec((1, vd), lambda b, h, qi, ki: (0, 0)),
        ],
        out_specs=pl.BlockSpec((tq, vd), lambda b, h, qi, ki: (b * nq + qi, h)),
        out_shape=jax.ShapeDtypeStruct((n, DIFF_HEADS * vd), bf16),
        scratch_shapes=[pltpu.VMEM((2, tq, 1), f32), pltpu.VMEM((2, tq, 1), f32),
                        pltpu.VMEM((2, tq, vd), f32)],
        compiler_params=_cparams(4),
        name="diff_attention",
    )(proj, proj, proj, lam_p, subln_g.reshape(1, vd))


def _loga_kernel(h_ref, w1_ref, w2_ref, b_ref, o_ref):
    g1 = jnp.dot(h_ref[...], w1_ref[...].astype(bf16), preferred_element_type=f32)
    gpre = jnp.dot(g1.astype(bf16), w2_ref[...].astype(bf16), preferred_element_type=f32) + b_ref[...]
    nx = -gpre
    softplus = jnp.maximum(nx, 0.0) + jnp.log1p(jnp.exp(-jnp.abs(nx)))
    o_ref[...] = -softplus / GLA_TAU


def _log_decay(h, w_g1, w2, bias):
    n, d = h.shape
    kdim = w2.shape[1]
    bm = min(512, n)
    w1p = jnp.zeros((d, LANES), f32).at[:, :GLA_RANK].set(w_g1)
    w2p = jnp.zeros((LANES, kdim), f32).at[:GLA_RANK].set(w2)
    return pl.pallas_call(
        _loga_kernel,
        grid=(n // bm,),
        in_specs=[
            pl.BlockSpec((bm, d), lambda i: (i, 0)),
            pl.BlockSpec((d, LANES), lambda i: (0, 0)),
            pl.BlockSpec((LANES, kdim), lambda i: (0, 0)),
            pl.BlockSpec((1, kdim), lambda i: (0, 0)),
        ],
        out_specs=pl.BlockSpec((bm, kdim), lambda i: (i, 0)),
        out_shape=jax.ShapeDtypeStruct((n, kdim), f32),
        compiler_params=_cparams(1),
        name="gla_log_decay",
    )(h, w1p, w2p, bias.reshape(1, kdim))


def _gla_kernel(q_ref, k_ref, v_ref, la_ref, r_ref, g_ref, o_ref, st_ref, *, scale):
    c = pl.program_id(2)

    @pl.when(c == 0)
    def _():
        st_ref[...] = jnp.zeros(st_ref.shape, f32)

    ch = q_ref.shape[0]
    dv = v_ref.shape[1]
    la = la_ref[...]
    row = lax.broadcasted_iota(jnp.int32, (ch, ch), 0)
    col = lax.broadcasted_iota(jnp.int32, (ch, ch), 1)
    causal = row >= col
    hi = lax.Precision.HIGHEST
    b = jnp.dot(causal.astype(f32), la, preferred_element_type=f32, precision=hi)
    b_last = b[ch - 1:ch, :]
    b_last_col = lax.dot_general(la, jnp.ones((ch, LANES), f32), (((0,), (0,)), ((), ())),
                                 preferred_element_type=f32, precision=hi)
    q = q_ref[...].astype(f32) * scale
    k = k_ref[...].astype(f32)
    q_t = (q * jnp.exp(b)).astype(bf16)
    k_t = (k * jnp.exp(-b)).astype(bf16)
    k_dec = (k * jnp.exp(b_last - b)).astype(bf16)
    v = v_ref[...]
    attn = lax.dot_general(q_t, k_t, (((1,), (1,)), ((), ())), preferred_element_type=f32)
    attn = jnp.where(causal, attn, 0.0)
    st = st_ref[...]
    o = (jnp.dot(attn.astype(bf16), v, preferred_element_type=f32)
         + jnp.dot(q_t, st.astype(bf16), preferred_element_type=f32))
    kv = lax.dot_general(k_dec, v, (((0,), (0,)), ((), ())), preferred_element_type=f32)
    st_ref[...] = jnp.tile(jnp.exp(b_last_col), (1, dv // LANES)) * st + kv
    o = o * lax.rsqrt(jnp.mean(o * o, axis=-1, keepdims=True) + NORM_EPS) * g_ref[...]
    o_ref[...] = (o * _silu(r_ref[...].astype(f32))).astype(o_ref.dtype)


def _gla(proj, log_a, norm_g, bsz, seq, dk, dv):
    n = proj.shape[0]
    ch = GLA_CHUNK
    nc = seq // ch
    kcols = GLA_HEADS * dk
    vcols = GLA_HEADS * dv
    rowb = lambda b, h, c: b * nc + c
    kern = functools.partial(_gla_kernel, scale=dk ** -0.5)
    return pl.pallas_call(
        kern,
        grid=(bsz, GLA_HEADS, nc),
        in_specs=[
            pl.BlockSpec((ch, dk), lambda b, h, c: (rowb(b, h, c), h)),
            pl.BlockSpec((ch, dk), lambda b, h, c: (rowb(b, h, c), kcols // dk + h)),
            pl.BlockSpec((ch, dv), lambda b, h, c: (rowb(b, h, c), 2 * kcols // dv + h)),
            pl.BlockSpec((ch, dk), lambda b, h, c: (rowb(b, h, c), h)),
            pl.BlockSpec((ch, dv), lambda b, h, c: (rowb(b, h, c), (2 * kcols + vcols) // dv + h)),
            pl.BlockSpec((1, dv), lambda b, h, c: (0, 0)),
        ],
        out_specs=pl.BlockSpec((ch, dv), lambda b, h, c: (rowb(b, h, c), h)),
        out_shape=jax.ShapeDtypeStruct((n, vcols), bf16),
        scratch_shapes=[pltpu.VMEM((dk, dv), f32)],
        compiler_params=_cparams(3),
        name="gla_chunked",
    )(proj, proj, proj, log_a, proj, norm_g.reshape(1, dv))


def _row_copy(src_hbm, dst_vmem, sem, src_row, dst_row):
    return pltpu.make_async_copy(src_hbm.at[pl.ds(src_row, 1)], dst_vmem.at[pl.ds(dst_row, 1)], sem)


def _gather_kernel(tok_ref, h_hbm, o_ref, buf, sem, *, tr):
    def issue(r, carry):
        _row_copy(h_hbm, buf, sem, tok_ref[0, r], r).start()
        return carry

    def wait(r, carry):
        _row_copy(h_hbm, buf, sem, 0, r).wait()
        return carry

    lax.fori_loop(0, tr, issue, 0)
    lax.fori_loop(0, tr, wait, 0)
    o_ref[...] = buf[...].astype(o_ref.dtype)


def _gather_rows(h, row_token, tr):
    n, d = h.shape
    r = row_token.shape[0]
    kern = functools.partial(_gather_kernel, tr=tr)
    return pl.pallas_call(
        kern,
        grid=(r // tr,),
        in_specs=[
            pl.BlockSpec((None, 1, tr), lambda i: (i, 0, 0), memory_space=pltpu.SMEM),
            pl.BlockSpec(memory_space=pl.ANY),
        ],
        out_specs=pl.BlockSpec((tr, d), lambda i: (i, 0)),
        out_shape=jax.ShapeDtypeStruct((r, d), bf16),
        scratch_shapes=[pltpu.VMEM((tr, d), f32), pltpu.SemaphoreType.DMA(())],
        compiler_params=_cparams(1),
        name="moe_gather",
    )(row_token.reshape(r // tr, 1, tr), h)


def _combine_kernel(pos_ref, y_hbm, x_ref, gate_ref, fn_ref, o_ref, buf, sem, *, tt):
    def issue(r, carry):
        for kk in range(TOP_K):
            _row_copy(y_hbm, buf.at[kk], sem, pos_ref[0, TOP_K * r + kk], r).start()
        return carry

    def wait(r, carry):
        for kk in range(TOP_K):
            _row_copy(y_hbm, buf.at[kk], sem, 0, r).wait()
        return carry

    lax.fori_loop(0, tt, issue, 0)
    lax.fori_loop(0, tt, wait, 0)
    moe = buf[0] + buf[1]
    x = x_ref[...] + gate_ref[...] * moe
    y = x * lax.rsqrt(jnp.mean(x * x, axis=-1, keepdims=True) + NORM_EPS)
    o_ref[...] = y * fn_ref[...]


def _combine(y, pos, x, mod, i_gate, final_norm, seq):
    n, d = x.shape
    tt = min(128, seq)
    kern = functools.partial(_combine_kernel, tt=tt)
    return pl.pallas_call(
        kern,
        grid=(n // tt,),
        in_specs=[
            pl.BlockSpec((None, 1, TOP_K * tt), lambda i: (i, 0, 0), memory_space=pltpu.SMEM),
            pl.BlockSpec(memory_space=pl.ANY),
            pl.BlockSpec((tt, d), lambda i: (i, 0)),
            pl.BlockSpec((None, None, 1, d), lambda i: (i * tt // seq, i_gate, 0, 0)),
            pl.BlockSpec((1, d), lambda i: (0, 0)),
        ],
        out_specs=pl.BlockSpec((tt, d), lambda i: (i, 0)),
        out_shape=jax.ShapeDtypeStruct((n, d), f32),
        scratch_shapes=[pltpu.VMEM((TOP_K, tt, d), f32), pltpu.SemaphoreType.DMA(())],
        compiler_params=_cparams(1),
        name="moe_combine_final_norm",
    )(pos.reshape(n // tt, 1, TOP_K * tt), y, x, mod, final_norm.reshape(1, d))


def _routing_tables(top_i, bm):
    n = top_i.shape[0]
    e_flat = top_i.reshape(-1)
    onehot = (e_flat[:, None] == jnp.arange(N_EXPERTS)[None, :]).astype(jnp.int32)
    counts = jnp.sum(onehot, axis=0)
    rank = jnp.take_along_axis(jnp.cumsum(onehot, axis=0) - onehot, e_flat[:, None], axis=1)[:, 0]
    padded = ((counts + bm - 1) // bm) * bm
    ends = jnp.cumsum(padded)
    starts = ends - padded
    pos = starts[e_flat] + rank
    n_rows = n * TOP_K + N_EXPERTS * bm
    n_tiles = n_rows // bm
    num_tiles = (ends[-1] // bm).astype(jnp.int32)
    tile_start = jnp.minimum(jnp.arange(n_tiles), num_tiles - 1) * bm
    tile_group = jnp.minimum(jnp.searchsorted(ends, tile_start, side="right"),
                             N_EXPERTS - 1).astype(jnp.int32)
    row_token = jnp.zeros((n_rows,), jnp.int32).at[pos].set(jnp.arange(n * TOP_K, dtype=jnp.int32) // TOP_K)
    return pos.astype(jnp.int32), row_token, tile_group, num_tiles.reshape(1)


def _rope_tables(seq, hd):
    inv = ROPE_THETA ** (-jnp.arange(0, hd, 2, dtype=f32) / hd)
    ang = jnp.arange(seq).astype(f32)[:, None] * inv[None, :]
    cos, sin = jnp.cos(ang), jnp.sin(ang)
    return jnp.concatenate([cos, cos], axis=-1), jnp.concatenate([-sin, sin], axis=-1)


def kernel(x, c, norm_gains, ada_w, ada_b, e_w_in, e_conv_w, e_conv_b, e_conv_ln_g, e_conv_ln_b, e_diff_lambda, e_diff_subln, e_w_out, e_ffn_gate, e_ffn_up, e_ffn_down, o_w_in, o_gate_w2, o_gate_b, o_gla_norm, o_w_out, o_router, o_exp_gate, o_exp_up, o_exp_down, final_norm):
    bsz, seq, d = x.shape
    n = bsz * seq
    depth = ada_w.shape[0]
    assert depth == 2, "trunk is one even (conv + diff-attn) and one odd (GLA + experts) layer"
    xf = x.reshape(n, d)
    mods = _ada(c, ada_w, ada_b)
    bm = min(1024, seq)

    mod = mods[0]
    conv_ch = e_conv_w.shape[2]
    hd = e_diff_lambda.shape[2]
    qk_cols = 2 * DIFF_HEADS * hd
    q_col0 = 2 * conv_ch
    k_col0 = q_col0 + qk_cols
    v_col0 = k_col0 + qk_cols
    even_in = e_w_in.shape[2]
    bn = 512
    h = _normmod(xf, norm_gains[0, 0], mod, 0, 1, seq)
    cos_t, sin_t = _rope_tables(seq, hd)
    rope_spec = pl.BlockSpec(
        (bm, hd), lambda j, i, te, nt: (jnp.minimum(i, nt[0] - 1) % (seq // bm), 0))
    ep = functools.partial(_ep_rope, q_tiles=(q_col0 // bn, k_col0 // bn),
                           k_tiles=(k_col0 // bn, v_col0 // bn), q_scale=hd ** -0.5)
    proj = _gmm([h], [e_w_in], bm=bm, bn=bn, n_cols=even_in, epilogue=ep,
                extras=(cos_t, sin_t), extra_specs=(rope_spec, rope_spec),
                out_dtype=bf16, name="even_in_proj")
    y_a = _conv(proj, e_conv_w[0], e_conv_b[0], e_conv_ln_g[0], e_conv_ln_b[0], seq)
    lambda_init = 0.8 - 0.6 * math.exp(-0.3 * 0)
    y_b = _diff_attention(proj, e_diff_lambda[0], e_diff_subln[0], bsz, seq,
                          q_col0, k_col0, v_col0, lambda_init)
    xf = _gmm([y_a, y_b], [e_w_out], bm=bm, bn=bn, n_cols=d, epilogue=_ep_resid,
              extras=(xf, mod), extra_specs=(_tile_spec(bm, bn), _gate_spec(bm, bn, seq, 2)),
              out_dtype=f32, name="even_out_proj")
    h = _normmod(xf, norm_gains[0, 1], mod, 3, 4, seq)
    d_ff = e_ffn_gate.shape[2]
    bnf = 256
    act = _gmm([h], [e_ffn_gate, e_ffn_up], bm=bm, bn=bnf, n_cols=d_ff, epilogue=_ep_swiglu,
               out_dtype=bf16, name="ffn_gate_up")
    half = d_ff // 2
    bmd = min(512, seq)
    part = _gmm([act], [e_ffn_down], bm=bmd, bn=bn, n_cols=d, k_block=half, k_index=0,
                epilogue=_ep_plain, out_dtype=f32, name="ffn_down_lo")
    xf = _gmm([act], [e_ffn_down], bm=bmd, bn=bn, n_cols=d, k_block=half, k_index=1,
              epilogue=_ep_resid_partial,
              extras=(xf, mod, part),
              extra_specs=(_tile_spec(bmd, bn), _gate_spec(bmd, bn, seq, 5), _tile_spec(bmd, bn)),
              out_dtype=f32, name="ffn_down_hi")

    mod = mods[1]
    dk = o_gate_w2.shape[2] // GLA_HEADS
    dv = o_gla_norm.shape[1]
    kcols = GLA_HEADS * dk
    vcols = GLA_HEADS * dv
    main_cols = 2 * kcols + 2 * vcols
    h = _normmod(xf, norm_gains[1, 0], mod, 0, 1, seq)
    proj = _gmm([h], [o_w_in], bm=bm, bn=bn, n_cols=main_cols, epilogue=_ep_plain,
                out_dtype=bf16, name="odd_in_proj")
    log_a = _log_decay(h, o_w_in[0, :, main_cols:], o_gate_w2[0], o_gate_b[0])
    o = _gla(proj, log_a, o_gla_norm[0], bsz, seq, dk, dv)
    xf = _gmm([o], [o_w_out], bm=bm, bn=bn, n_cols=d, epilogue=_ep_resid,
              extras=(xf, mod), extra_specs=(_tile_spec(bm, bn), _gate_spec(bm, bn, seq, 2)),
              out_dtype=f32, name="odd_out_proj")
    h32, top_i, top_w = _normmod_router(xf, norm_gains[1, 1], mod, 3, 4, seq, o_router[0])
    bme = min(512, seq)
    pos, row_token, tile_group, num_tiles = _routing_tables(top_i, bme)
    n_rows = row_token.shape[0]
    hs = _gather_rows(h32, row_token, min(256, bme))
    d_fe = o_exp_gate.shape[3]
    act = _gmm([hs], [o_exp_gate[0], o_exp_up[0]], bm=bme, bn=bnf, n_cols=d_fe,
               epilogue=_ep_swiglu, out_dtype=bf16, tile_group=tile_group, num_tiles=num_tiles,
               name="expert_gate_up")
    row_w = jnp.zeros((n_rows, 1), f32).at[pos, 0].set(top_w.reshape(-1))
    rs_spec = pl.BlockSpec((bme, 1), lambda j, i, te, nt: (jnp.minimum(i, nt[0] - 1), 0))
    ye = _gmm([act], [o_exp_down[0]], bm=bme, bn=bn, n_cols=d, epilogue=_ep_rowscale,
              extras=(row_w,), extra_specs=(rs_spec,), out_dtype=f32,
              tile_group=tile_group, num_tiles=num_tiles, name="expert_down")
    out = _combine(ye, pos, xf, mod, 5, final_norm, seq)
    return out.reshape(bsz, seq, d)
```

```python
import functools
import math

import jax
import jax.numpy as jnp
from jax import lax
from jax.experimental import pallas as pl
from jax.experimental.pallas import tpu as pltpu

f32 = jnp.float32
bf16 = jnp.bfloat16

N_MOD = 6
NORM_EPS = 1e-6
CONV_WIDTH = 31
CONV_HALO = 32
DIFF_HEADS = 8
DIFF_SUBLN_EPS = 1e-5
ROPE_THETA = 10000.0
GLA_HEADS = 4
GLA_RANK = 16
GLA_TAU = 16.0
GLA_CHUNK = 64
N_EXPERTS = 8
TOP_K = 2
LANES = 128
VMEM_LIMIT_BYTES = 58 * 1024 * 1024


def _cparams(n_axes):
    return pltpu.CompilerParams(
        dimension_semantics=("arbitrary",) * n_axes,
        vmem_limit_bytes=VMEM_LIMIT_BYTES)


def _sigmoid(x):
    return 1.0 / (1.0 + jnp.exp(-x))


def _silu(x):
    return x * _sigmoid(x)


def _ada_kernel(c_ref, w_ref, b_ref, o_ref):
    ca = _silu(c_ref[...]).astype(bf16)
    w = w_ref[...].astype(bf16)
    o_ref[...] = jnp.dot(ca, w, preferred_element_type=f32) + b_ref[...]


def _ada(c, ada_w, ada_b):
    depth, d, n6 = ada_w.shape
    bsz = c.shape[0]
    rows = 8
    cp = jnp.zeros((rows, d), f32).at[:bsz].set(c)
    bn = min(512, n6)
    out = pl.pallas_call(
        _ada_kernel,
        grid=(depth, n6 // bn),
        in_specs=[
            pl.BlockSpec((rows, d), lambda l, j: (0, 0)),
            pl.BlockSpec((None, d, bn), lambda l, j: (l, 0, j)),
            pl.BlockSpec((None, 1, bn), lambda l, j: (l, 0, j)),
        ],
        out_specs=pl.BlockSpec((None, rows, bn), lambda l, j: (l, 0, j)),
        out_shape=jax.ShapeDtypeStruct((depth, rows, n6), f32),
        compiler_params=_cparams(2),
        name="ada_mod",
    )(cp, ada_w, ada_b.reshape(depth, 1, n6))
    return out[:, :bsz].reshape(depth, bsz, N_MOD, 1, d)


def _normmod_value(x_ref, g_ref, sh_ref, sc_ref):
    x = x_ref[...]
    y = x * lax.rsqrt(jnp.mean(x * x, axis=-1, keepdims=True) + NORM_EPS)
    y = y * g_ref[...]
    return y * (1.0 + sc_ref[...]) + sh_ref[...]


def _normmod_kernel(x_ref, g_ref, sh_ref, sc_ref, o_ref):
    o_ref[...] = _normmod_value(x_ref, g_ref, sh_ref, sc_ref).astype(o_ref.dtype)


def _normmod_specs(bm, d, seq, i_shift, i_scale):
    return [
        pl.BlockSpec((bm, d), lambda i: (i, 0)),
        pl.BlockSpec((1, d), lambda i: (0, 0)),
        pl.BlockSpec((None, None, 1, d), lambda i: (i * bm // seq, i_shift, 0, 0)),
        pl.BlockSpec((None, None, 1, d), lambda i: (i * bm // seq, i_scale, 0, 0)),
    ]


def _normmod(x, gain, mod, i_shift, i_scale, seq):
    n, d = x.shape
    bm = min(512, seq)
    return pl.pallas_call(
        _normmod_kernel,
        grid=(n // bm,),
        in_specs=_normmod_specs(bm, d, seq, i_shift, i_scale),
        out_specs=pl.BlockSpec((bm, d), lambda i: (i, 0)),
        out_shape=jax.ShapeDtypeStruct((n, d), bf16),
        compiler_params=_cparams(1),
        name="normmod",
    )(x, gain.reshape(1, d), mod, mod)


def _store_slabs(o_ref, val):
    for a in range(val.shape[1] // LANES):
        o_ref[:, a, :] = val[:, a * LANES:(a + 1) * LANES]


def _load_slabs(ref):
    return jnp.concatenate([ref[:, a, :] for a in range(ref.shape[1])], axis=1)


def _normmod_router_kernel(x_ref, g_ref, sh_ref, sc_ref, wr_ref, h_ref, idx_ref, wgt_ref):
    h = _normmod_value(x_ref, g_ref, sh_ref, sc_ref)
    _store_slabs(h_ref, h)
    logits = jnp.dot(h, wr_ref[...], preferred_element_type=f32,
                     precision=lax.Precision.HIGHEST)
    col = lax.broadcasted_iota(jnp.int32, logits.shape, 1)
    neg = jnp.float32(-jnp.inf)
    l0 = jnp.where(col < N_EXPERTS, logits, neg)
    m1 = jnp.max(l0, axis=-1, keepdims=True)
    i1 = jnp.min(jnp.where(l0 == m1, col, LANES), axis=-1, keepdims=True)
    l1 = jnp.where(col == i1, neg, l0)
    m2 = jnp.max(l1, axis=-1, keepdims=True)
    i2 = jnp.min(jnp.where(l1 == m2, col, LANES), axis=-1, keepdims=True)
    e = jnp.exp(m2 - m1)
    w1 = 1.0 / (1.0 + e)
    w2 = e / (1.0 + e)
    idx_ref[...] = jnp.where(col == 0, i1, jnp.where(col == 1, i2, 0))
    wgt_ref[...] = jnp.where(col == 0, w1, jnp.where(col == 1, w2, 0.0))


def _normmod_router(x, gain, mod, i_shift, i_scale, seq, w_router):
    n, d = x.shape
    bm = min(256, seq)
    wr = jnp.zeros((d, LANES), f32).at[:, :N_EXPERTS].set(w_router)
    h, idx, wgt = pl.pallas_call(
        _normmod_router_kernel,
        grid=(n // bm,),
        in_specs=_normmod_specs(bm, d, seq, i_shift, i_scale)
        + [pl.BlockSpec((d, LANES), lambda i: (0, 0))],
        out_specs=[
            pl.BlockSpec((bm, d // LANES, LANES), lambda i: (i, 0, 0)),
            pl.BlockSpec((bm, LANES), lambda i: (i, 0)),
            pl.BlockSpec((bm, LANES), lambda i: (i, 0)),
        ],
        out_shape=[
            jax.ShapeDtypeStruct((n, d // LANES, LANES), f32),
            jax.ShapeDtypeStruct((n, LANES), jnp.int32),
            jax.ShapeDtypeStruct((n, LANES), f32),
        ],
        compiler_params=_cparams(1),
        name="normmod_router",
    )(x, gain.reshape(1, d), mod, mod, wr)
    return h, idx[:, :TOP_K], wgt


def _gmm_kernel(te_ref, nt_ref, *refs, n_x, n_w, n_extra, k_parts, epilogue):
    x_refs = refs[:n_x]
    w_refs = refs[n_x:n_x + n_w]
    e_refs = refs[n_x + n_w:n_x + n_w + n_extra]
    o_ref = refs[n_x + n_w + n_extra]
    wb_ref = refs[n_x + n_w + n_extra + 1]
    i = pl.program_id(1)
    active = i < nt_ref[0]
    prev = te_ref[jnp.maximum(i - 1, 0)]
    changed = jnp.logical_and(active, jnp.logical_or(i == 0, te_ref[i] != prev))

    @pl.when(changed)
    def _():
        for t in range(n_w):
            wb_ref[t] = w_refs[t][...].astype(bf16)

    @pl.when(active)
    def _():
        accs = []
        for t in range(n_w):
            acc = None
            off = 0
            for p in range(n_x):
                kp = k_parts[p]
                part = jnp.dot(x_refs[p][...], wb_ref[t, off:off + kp, :],
                               preferred_element_type=f32)
                acc = part if acc is None else acc + part
                off += kp
            accs.append(acc)
        epilogue(accs, e_refs, o_ref)

    @pl.when(jnp.logical_not(active))
    def _():
        o_ref[...] = jnp.zeros(o_ref.shape, o_ref.dtype)


def _gmm(xs, ws, *, bm, bn, n_cols, k_block=None, k_index=0, epilogue, extras=(), extra_specs=(),
         out_dtype, tile_group=None, num_tiles=None, slab_out=False, name):
    m = xs[0].shape[0]
    k_parts = tuple(x.shape[1] for x in xs) if k_block is None else (k_block,)
    k = sum(k_parts)
    n_tiles = m // bm
    if tile_group is None:
        tile_group = jnp.zeros((n_tiles,), jnp.int32)
        num_tiles = jnp.full((1,), n_tiles, jnp.int32)
    xk = k_index if k_block is not None else 0

    def row(i, nt):
        return jnp.minimum(i, nt[0] - 1)

    in_specs = [pl.BlockSpec((bm, kp), lambda j, i, te, nt: (row(i, nt), xk)) for kp in k_parts]
    in_specs += [pl.BlockSpec((None, k, bn), lambda j, i, te, nt: (te[row(i, nt)], xk, j))
                 for _ in ws]
    in_specs += list(extra_specs)
    kern = functools.partial(_gmm_kernel, n_x=len(xs), n_w=len(ws), n_extra=len(extras),
                             k_parts=k_parts, epilogue=epilogue)
    if slab_out:
        out_spec = pl.BlockSpec((bm, bn // LANES, LANES), lambda j, i, te, nt: (i, j, 0))
        out_shape = jax.ShapeDtypeStruct((m, n_cols // LANES, LANES), out_dtype)
    else:
        out_spec = pl.BlockSpec((bm, bn), lambda j, i, te, nt: (i, j))
        out_shape = jax.ShapeDtypeStruct((m, n_cols), out_dtype)
    return pl.pallas_call(
        kern,
        grid_spec=pltpu.PrefetchScalarGridSpec(
            num_scalar_prefetch=2,
            grid=(n_cols // bn, n_tiles),
            in_specs=in_specs,
            out_specs=out_spec,
            scratch_shapes=[pltpu.VMEM((len(ws), k, bn), bf16)],
        ),
        out_shape=out_shape,
        compiler_params=_cparams(2),
        name=name,
    )(tile_group, num_tiles, *xs, *ws, *extras)


def _ep_plain(accs, e_refs, o_ref):
    o_ref[...] = accs[0].astype(o_ref.dtype)


def _ep_swiglu(accs, e_refs, o_ref):
    g, u = accs
    o_ref[...] = (_silu(g) * u).astype(o_ref.dtype)


def _ep_resid(accs, e_refs, o_ref):
    res_ref, gate_ref = e_refs
    o_ref[...] = res_ref[...] + gate_ref[...] * accs[0]


def _ep_resid_partial(accs, e_refs, o_ref):
    res_ref, gate_ref, part_ref = e_refs
    o_ref[...] = res_ref[...] + gate_ref[...] * (part_ref[...] + accs[0])


def _ep_slabs(accs, e_refs, o_ref):
    _store_slabs(o_ref, accs[0])


def _ep_rope(accs, e_refs, o_ref, *, q_tiles, k_tiles, q_scale):
    cos_ref, sin_ref = e_refs
    acc = accs[0]
    j = pl.program_id(0)
    is_q = jnp.logical_and(j >= q_tiles[0], j < q_tiles[1])
    is_k = jnp.logical_and(j >= k_tiles[0], j < k_tiles[1])

    def roped(scale):
        cos = cos_ref[...]
        sin = sin_ref[...]
        outs = []
        for hh in range(acc.shape[1] // LANES):
            a = acc[:, hh * LANES:(hh + 1) * LANES]
            rot = pltpu.roll(a, LANES // 2, 1)
            outs.append((a * cos + rot * sin) * scale)
        return jnp.concatenate(outs, axis=1)

    @pl.when(is_q)
    def _():
        o_ref[...] = roped(q_scale).astype(o_ref.dtype)

    @pl.when(is_k)
    def _():
        o_ref[...] = roped(1.0).astype(o_ref.dtype)

    @pl.when(jnp.logical_not(jnp.logical_or(is_q, is_k)))
    def _():
        o_ref[...] = acc.astype(o_ref.dtype)


def _gate_spec(bm, bn, seq, i_gate):
    return pl.BlockSpec((None, None, 1, bn),
                        lambda j, i, te, nt: (jnp.minimum(i, nt[0] - 1) * bm // seq, i_gate, 0, j))


def _tile_spec(bm, bn):
    return pl.BlockSpec((bm, bn), lambda j, i, te, nt: (jnp.minimum(i, nt[0] - 1), j))


def _conv_kernel(val_ref, gate_ref, hval_ref, hgate_ref, w_ref, b_ref, lg_ref, lb_ref,
                 o_ref, u_scr, v_scr, *, ts, seq, rows):
    i = pl.program_id(0)
    first = (i * ts) % seq == 0
    hu = hval_ref[...].astype(f32) * _sigmoid(hgate_ref[...].astype(f32))
    u_scr[0:CONV_HALO, :] = jnp.where(first, 0.0, hu)
    u_scr[CONV_HALO:, :] = val_ref[...].astype(f32) * _sigmoid(gate_ref[...].astype(f32))
    lead = CONV_HALO - (CONV_WIDTH - 1)

    cw = 2 * LANES

    def chunk(c, carry):
        r0 = pl.multiple_of(c * rows, rows)
        for c0 in range(0, u_scr.shape[1], cw):
            win = u_scr[pl.ds(r0, rows + CONV_HALO), c0:c0 + cw]
            acc = jnp.zeros((rows, cw), f32)
            for j in range(CONV_WIDTH):
                acc = acc + w_ref[j:j + 1, c0:c0 + cw] * win[lead + j:lead + j + rows, :]
            v_scr[pl.ds(r0, rows), c0:c0 + cw] = acc
        return carry

    lax.fori_loop(0, ts // rows, chunk, 0)
    u = v_scr[...] + b_ref[...]
    mu = jnp.mean(u, axis=-1, keepdims=True)
    var = jnp.mean(jnp.square(u - mu), axis=-1, keepdims=True)
    y = (u - mu) * lax.rsqrt(var + NORM_EPS) * lg_ref[...] + lb_ref[...]
    o_ref[...] = _silu(y).astype(o_ref.dtype)


def _conv(proj, conv_w, conv_b, ln_g, ln_b, seq):
    n = proj.shape[0]
    ch = conv_w.shape[1]
    ts = min(256, seq)
    hb = ts // CONV_HALO
    halo_row = lambda i: jnp.maximum(i * hb - 1, 0)
    vec = lambda: pl.BlockSpec((1, ch), lambda i: (0, 0))
    kern = functools.partial(_conv_kernel, ts=ts, seq=seq, rows=32)
    return pl.pallas_call(
        kern,
        grid=(n // ts,),
        in_specs=[
            pl.BlockSpec((ts, ch), lambda i: (i, 0)),
            pl.BlockSpec((ts, ch), lambda i: (i, 1)),
            pl.BlockSpec((CONV_HALO, ch), lambda i: (halo_row(i), 0)),
            pl.BlockSpec((CONV_HALO, ch), lambda i: (halo_row(i), 1)),
            pl.BlockSpec((CONV_WIDTH, ch), lambda i: (0, 0)),
            vec(), vec(), vec(),
        ],
        out_specs=pl.BlockSpec((ts, ch), lambda i: (i, 0)),
        out_shape=jax.ShapeDtypeStruct((n, ch), bf16),
        scratch_shapes=[pltpu.VMEM((ts + CONV_HALO, ch), f32), pltpu.VMEM((ts, ch), f32)],
        compiler_params=_cparams(1),
        name="conformer_conv",
    )(proj, proj, proj, proj, conv_w, conv_b.reshape(1, ch), ln_g.reshape(1, ch), ln_b.reshape(1, ch))


def _attn_kernel(qt_ref, kt_ref, q_ref, k_ref, v_ref, lam_ref, g_ref, o_ref, m_scr, l_scr, acc_scr,
                 *, lambda_init, hd):
    t = pl.program_id(2)
    qi = qt_ref[t]
    ki = kt_ref[t]

    @pl.when(ki == 0)
    def _():
        m_scr[...] = jnp.full(m_scr.shape, -jnp.inf, f32)
        l_scr[...] = jnp.zeros(l_scr.shape, f32)
        acc_scr[...] = jnp.zeros(acc_scr.shape, f32)

    def lane_fold(x, op):
        out = x[:, 0:LANES]
        for cb in range(1, x.shape[1] // LANES):
            out = op(out, x[:, cb * LANES:(cb + 1) * LANES])
        return out

    def step(masked):
        v = v_ref[...]
        for s in range(2):
            q = q_ref[:, s * hd:(s + 1) * hd]
            k = k_ref[:, s * hd:(s + 1) * hd]
            sc = lax.dot_general(q, k, (((1,), (1,)), ((), ())), preferred_element_type=f32)
            if masked:
                row = lax.broadcasted_iota(jnp.int32, sc.shape, 0)
                col = lax.broadcasted_iota(jnp.int32, sc.shape, 1)
                sc = jnp.where(col <= row, sc, -jnp.inf)
            m_cur = jnp.max(lane_fold(sc, jnp.maximum), axis=-1, keepdims=True)
            m_prev = m_scr[s]
            m_new = jnp.maximum(m_prev, m_cur)
            alpha = jnp.exp2(m_prev - m_new)
            p = jnp.exp2(sc - jnp.tile(m_new, (1, sc.shape[1] // LANES)))
            l_scr[s] = alpha * l_scr[s] + lane_fold(p, jnp.add)
            acc_scr[s] = (jnp.tile(alpha, (1, v.shape[1] // LANES)) * acc_scr[s]
                          + jnp.dot(p.astype(bf16), v, preferred_element_type=f32))
            m_scr[s] = m_new

    @pl.when(ki < qi)
    def _():
        step(False)

    @pl.when(ki == qi)
    def _():
        step(True)
        lp = lam_ref[...]
        lam = (jnp.exp(jnp.sum(lp[0:1] * lp[1:2], axis=-1, keepdims=True))
               - jnp.exp(jnp.sum(lp[2:3] * lp[3:4], axis=-1, keepdims=True)) + lambda_init)
        l0 = jnp.sum(l_scr[0], axis=-1, keepdims=True)
        l1 = jnp.sum(l_scr[1], axis=-1, keepdims=True)
        o = acc_scr[0] / l0 - lam * (acc_scr[1] / l1)
        o = o * lax.rsqrt(jnp.mean(o * o, axis=-1, keepdims=True) + DIFF_SUBLN_EPS)
        o_ref[...] = ((o * g_ref[...]) * (1.0 - lambda_init)).astype(o_ref.dtype)


def _diff_attention(proj, lam_p, subln_g, bsz, seq, q_col0, k_col0, v_col0, lambda_init):
    n = proj.shape[0]
    hd = lam_p.shape[1]
    vd = 2 * hd
    tq = min(512, seq)
    nq = seq // tq
    pairs = [(qi, ki) for qi in range(nq) for ki in range(qi + 1)]
    q_of = jnp.asarray([p[0] for p in pairs], jnp.int32)
    k_of = jnp.asarray([p[1] for p in pairs], jnp.int32)
    kern = functools.partial(_attn_kernel, lambda_init=lambda_init, hd=hd)
    return pl.pallas_call(
        kern,
        grid_spec=pltpu.PrefetchScalarGridSpec(
            num_scalar_prefetch=2,
            grid=(bsz, DIFF_HEADS, len(pairs)),
            in_specs=[
                pl.BlockSpec((tq, vd), lambda b, h, t, qt, kt: (b * nq + qt[t], q_col0 // vd + h)),
                pl.BlockSpec((tq, vd), lambda b, h, t, qt, kt: (b * nq + kt[t], k_col0 // vd + h)),
                pl.BlockSpec((tq, vd), lambda b, h, t, qt, kt: (b * nq + kt[t], v_col0 // vd + h)),
                pl.BlockSpec((4, hd), lambda b, h, t, qt, kt: (0, 0)),
                pl.BlockSpec((1, vd), lambda b, h, t, qt, kt: (0, 0)),
            ],
            out_specs=pl.BlockSpec((tq, vd), lambda b, h, t, qt, kt: (b * nq + qt[t], h)),
            scratch_shapes=[pltpu.VMEM((2, tq, LANES), f32), pltpu.VMEM((2, tq, LANES), f32),
                            pltpu.VMEM((2, tq, vd), f32)],
        ),
        out_shape=jax.ShapeDtypeStruct((n, DIFF_HEADS * vd), bf16),
        compiler_params=_cparams(3),
        name="diff_attention",
    )(q_of, k_of, proj, proj, proj, lam_p, subln_g.reshape(1, vd))


def _loga_kernel(h_ref, w1_ref, w2_ref, b_ref, o_ref):
    lane = lax.broadcasted_iota(jnp.int32, w1_ref.shape, 1)
    w1 = jnp.where(lane < GLA_RANK, w1_ref[...], 0.0).astype(bf16)
    g1 = jnp.dot(h_ref[...], w1, preferred_element_type=f32)
    gpre = jnp.dot(g1.astype(bf16), w2_ref[...].astype(bf16), preferred_element_type=f32) + b_ref[...]
    nx = -gpre
    softplus = jnp.maximum(nx, 0.0) + jnp.log1p(jnp.exp(-jnp.abs(nx)))
    o_ref[...] = -softplus / GLA_TAU


def _log_decay(h, w_in, col0, w2, bias):
    n, d = h.shape
    kdim = w2.shape[1]
    bm = min(512, n)
    w2p = jnp.zeros((LANES, kdim), f32).at[:GLA_RANK].set(w2)
    return pl.pallas_call(
        _loga_kernel,
        grid=(n // bm,),
        in_specs=[
            pl.BlockSpec((bm, d), lambda i: (i, 0)),
            pl.BlockSpec((None, d, LANES), lambda i: (0, 0, col0 // LANES)),
            pl.BlockSpec((LANES, kdim), lambda i: (0, 0)),
            pl.BlockSpec((1, kdim), lambda i: (0, 0)),
        ],
        out_specs=pl.BlockSpec((bm, kdim), lambda i: (i, 0)),
        out_shape=jax.ShapeDtypeStruct((n, kdim), f32),
        compiler_params=_cparams(1),
        name="gla_log_decay",
    )(h, w_in, w2p, bias.reshape(1, kdim))


def _gla_kernel(q_ref, k_ref, v_ref, la_ref, r_ref, g_ref, o_ref, st_ref, *, scale):
    c = pl.program_id(2)

    @pl.when(c == 0)
    def _():
        st_ref[...] = jnp.zeros(st_ref.shape, f32)

    ch = q_ref.shape[0]
    dv = v_ref.shape[1]
    la = la_ref[...]
    row = lax.broadcasted_iota(jnp.int32, (ch, ch), 0)
    col = lax.broadcasted_iota(jnp.int32, (ch, ch), 1)
    causal = row >= col
    hi = lax.Precision.HIGHEST
    b = jnp.dot(causal.astype(f32), la, preferred_element_type=f32, precision=hi)
    b_last = b[ch - 1:ch, :]
    b_mid = b[ch // 2 - 1:ch // 2, :]
    b_last_col = lax.dot_general(la, jnp.ones((ch, LANES), f32), (((0,), (0,)), ((), ())),
                                 preferred_element_type=f32, precision=hi)
    q = q_ref[...].astype(f32) * scale
    k = k_ref[...].astype(f32)
    q_t = (q * jnp.exp(b)).astype(bf16)
    q_rel = (q * jnp.exp(b - b_mid)).astype(bf16)
    k_rel = (k * jnp.exp(b_mid - b)).astype(bf16)
    k_dec = (k * jnp.exp(b_last - b)).astype(bf16)
    v = v_ref[...]
    attn = lax.dot_general(q_rel, k_rel, (((1,), (1,)), ((), ())), preferred_element_type=f32)
    attn = jnp.where(causal, attn, 0.0)
    st = st_ref[...]
    o = (jnp.dot(attn.astype(bf16), v, preferred_element_type=f32)
         + jnp.dot(q_t, st.astype(bf16), preferred_element_type=f32))
    kv = lax.dot_general(k_dec, v, (((0,), (0,)), ((), ())), preferred_element_type=f32)
    st_ref[...] = jnp.tile(jnp.exp(b_last_col), (1, dv // LANES)) * st + kv
    o = o * lax.rsqrt(jnp.mean(o * o, axis=-1, keepdims=True) + NORM_EPS) * g_ref[...]
    o_ref[...] = (o * _silu(r_ref[...].astype(f32))).astype(o_ref.dtype)


def _gla(proj, log_a, norm_g, bsz, seq, dk, dv):
    n = proj.shape[0]
    ch = 2 * GLA_CHUNK if seq % (2 * GLA_CHUNK) == 0 else GLA_CHUNK
    nc = seq // ch
    kcols = GLA_HEADS * dk
    vcols = GLA_HEADS * dv
    rowb = lambda b, h, c: b * nc + c
    kern = functools.partial(_gla_kernel, scale=dk ** -0.5)
    return pl.pallas_call(
        kern,
        grid=(bsz, GLA_HEADS, nc),
        in_specs=[
            pl.BlockSpec((ch, dk), lambda b, h, c: (rowb(b, h, c), h)),
            pl.BlockSpec((ch, dk), lambda b, h, c: (rowb(b, h, c), kcols // dk + h)),
            pl.BlockSpec((ch, dv), lambda b, h, c: (rowb(b, h, c), 2 * kcols // dv + h)),
            pl.BlockSpec((ch, dk), lambda b, h, c: (rowb(b, h, c), h)),
            pl.BlockSpec((ch, dv), lambda b, h, c: (rowb(b, h, c), (2 * kcols + vcols) // dv + h)),
            pl.BlockSpec((1, dv), lambda b, h, c: (0, 0)),
        ],
        out_specs=pl.BlockSpec((ch, dv), lambda b, h, c: (rowb(b, h, c), h)),
        out_shape=jax.ShapeDtypeStruct((n, vcols), bf16),
        scratch_shapes=[pltpu.VMEM((dk, dv), f32)],
        compiler_params=_cparams(3),
        name="gla_chunked",
    )(proj, proj, proj, log_a, proj, norm_g.reshape(1, dv))


GATHER_UNROLL = 8


def _slab_copy(src_hbm, dst_vmem, sem, src_row, dst_row):
    return pltpu.make_async_copy(src_hbm.at[pl.ds(src_row, 1)], dst_vmem.at[pl.ds(dst_row, 1)], sem)


def _gather_kernel(tok_ref, h_hbm, o_ref, buf, sem, *, tr):
    def issue(r, carry):
        _slab_copy(h_hbm, buf, sem, tok_ref[0, r], r).start()
        return carry

    def wait(r, carry):
        _slab_copy(h_hbm, buf, sem, 0, r).wait()
        return carry

    lax.fori_loop(0, tr, issue, 0, unroll=GATHER_UNROLL)
    lax.fori_loop(0, tr, wait, 0, unroll=GATHER_UNROLL)
    o_ref[...] = _load_slabs(buf).astype(o_ref.dtype)


def _gather_rows(h, row_token, tr):
    n, nslab, _ = h.shape
    r = row_token.shape[0]
    kern = functools.partial(_gather_kernel, tr=tr)
    return pl.pallas_call(
        kern,
        grid=(r // tr,),
        in_specs=[
            pl.BlockSpec((None, 1, tr), lambda i: (i, 0, 0), memory_space=pltpu.SMEM),
            pl.BlockSpec(memory_space=pl.ANY),
        ],
        out_specs=pl.BlockSpec((tr, nslab * LANES), lambda i: (i, 0)),
        out_shape=jax.ShapeDtypeStruct((r, nslab * LANES), bf16),
        scratch_shapes=[pltpu.VMEM((tr, nslab, LANES), f32), pltpu.SemaphoreType.DMA(())],
        compiler_params=_cparams(1),
        name="moe_gather",
    )(row_token.reshape(r // tr, 1, tr), h)


def _combine_kernel(pos_ref, y_hbm, w_ref, x_ref, gate_ref, fn_ref, o_ref, buf, sem, *, tt):
    def issue(r, carry):
        for kk in range(TOP_K):
            _slab_copy(y_hbm, buf.at[kk], sem, pos_ref[0, TOP_K * r + kk], r).start()
        return carry

    def wait(r, carry):
        for kk in range(TOP_K):
            _slab_copy(y_hbm, buf.at[kk], sem, 0, r).wait()
        return carry

    lax.fori_loop(0, tt, issue, 0, unroll=GATHER_UNROLL // TOP_K)
    lax.fori_loop(0, tt, wait, 0, unroll=GATHER_UNROLL // TOP_K)
    w = w_ref[...]
    moe = w[:, 0:1] * _load_slabs(buf.at[0]) + w[:, 1:2] * _load_slabs(buf.at[1])
    x = x_ref[...] + gate_ref[...] * moe
    y = x * lax.rsqrt(jnp.mean(x * x, axis=-1, keepdims=True) + NORM_EPS)
    o_ref[...] = y * fn_ref[...]


def _combine(y, pos, top_w, x, mod, i_gate, final_norm, seq):
    n, d = x.shape
    nslab = d // LANES
    tt = min(128, seq)
    kern = functools.partial(_combine_kernel, tt=tt)
    return pl.pallas_call(
        kern,
        grid=(n // tt,),
        in_specs=[
            pl.BlockSpec((None, 1, TOP_K * tt), lambda i: (i, 0, 0), memory_space=pltpu.SMEM),
            pl.BlockSpec(memory_space=pl.ANY),
            pl.BlockSpec((tt, LANES), lambda i: (i, 0)),
            pl.BlockSpec((tt, d), lambda i: (i, 0)),
            pl.BlockSpec((None, None, 1, d), lambda i: (i * tt // seq, i_gate, 0, 0)),
            pl.BlockSpec((1, d), lambda i: (0, 0)),
        ],
        out_specs=pl.BlockSpec((tt, d), lambda i: (i, 0)),
        out_shape=jax.ShapeDtypeStruct((n, d), f32),
        scratch_shapes=[pltpu.VMEM((TOP_K, tt, nslab, LANES), f32), pltpu.SemaphoreType.DMA(())],
        compiler_params=_cparams(1),
        name="moe_combine_final_norm",
    )(pos.reshape(n // tt, 1, TOP_K * tt), y, top_w, x, mod, final_norm.reshape(1, d))


def _routing_tables(top_i, bm):
    n = top_i.shape[0]
    e_flat = top_i.reshape(-1)
    onehot = (e_flat[:, None] == jnp.arange(N_EXPERTS)[None, :]).astype(jnp.int32)
    counts = jnp.sum(onehot, axis=0)
    rank = jnp.take_along_axis(jnp.cumsum(onehot, axis=0) - onehot, e_flat[:, None], axis=1)[:, 0]
    padded = ((counts + bm - 1) // bm) * bm
    ends = jnp.cumsum(padded)
    starts = ends - padded
    pos = starts[e_flat] + rank
    n_rows = n * TOP_K + N_EXPERTS * bm
    n_tiles = n_rows // bm
    num_tiles = (ends[-1] // bm).astype(jnp.int32)
    tile_start = jnp.minimum(jnp.arange(n_tiles), num_tiles - 1) * bm
    tile_group = jnp.minimum(jnp.searchsorted(ends, tile_start, side="right"),
                             N_EXPERTS - 1).astype(jnp.int32)
    row_token = jnp.zeros((n_rows,), jnp.int32).at[pos].set(jnp.arange(n * TOP_K, dtype=jnp.int32) // TOP_K)
    return pos.astype(jnp.int32), row_token, tile_group, num_tiles.reshape(1)


def _rope_tables(seq, hd):
    inv = ROPE_THETA ** (-jnp.arange(0, hd, 2, dtype=f32) / hd)
    ang = jnp.arange(seq).astype(f32)[:, None] * inv[None, :]
    cos, sin = jnp.cos(ang), jnp.sin(ang)
    return jnp.concatenate([cos, cos], axis=-1), jnp.concatenate([-sin, sin], axis=-1)


def kernel(x, c, norm_gains, ada_w, ada_b, e_w_in, e_conv_w, e_conv_b, e_conv_ln_g, e_conv_ln_b, e_diff_lambda, e_diff_subln, e_w_out, e_ffn_gate, e_ffn_up, e_ffn_down, o_w_in, o_gate_w2, o_gate_b, o_gla_norm, o_w_out, o_router, o_exp_gate, o_exp_up, o_exp_down, final_norm):
    bsz, seq, d = x.shape
    n = bsz * seq
    depth = ada_w.shape[0]
    assert depth == 2, "trunk is one even (conv + diff-attn) and one odd (GLA + experts) layer"
    xf = x.reshape(n, d)
    mods = _ada(c, ada_w, ada_b)
    bm = min(1024, seq)

    mod = mods[0]
    conv_ch = e_conv_w.shape[2]
    hd = e_diff_lambda.shape[2]
    qk_cols = 2 * DIFF_HEADS * hd
    q_col0 = 2 * conv_ch
    k_col0 = q_col0 + qk_cols
    v_col0 = k_col0 + qk_cols
    even_in = e_w_in.shape[2]
    bn = 512
    h = _normmod(xf, norm_gains[0, 0], mod, 0, 1, seq)
    cos_t, sin_t = _rope_tables(seq, hd)
    bmi = min(512, seq)
    bni = max(t for t in (1024, 512, 256)
              if all(cc % t == 0 for cc in (q_col0, k_col0, v_col0, even_in)))
    rope_spec = pl.BlockSpec(
        (bmi, hd), lambda j, i, te, nt: (jnp.minimum(i, nt[0] - 1) % (seq // bmi), 0))
    ep = functools.partial(_ep_rope, q_tiles=(q_col0 // bni, k_col0 // bni),
                           k_tiles=(k_col0 // bni, v_col0 // bni),
                           q_scale=hd ** -0.5 * math.log2(math.e))
    proj = _gmm([h], [e_w_in], bm=bmi, bn=bni, n_cols=even_in, epilogue=ep,
                extras=(cos_t, sin_t), extra_specs=(rope_spec, rope_spec),
                out_dtype=bf16, name="even_in_proj")
    y_a = _conv(proj, e_conv_w[0], e_conv_b[0], e_conv_ln_g[0], e_conv_ln_b[0], seq)
    lambda_init = 0.8 - 0.6 * math.exp(-0.3 * 0)
    y_b = _diff_attention(proj, e_diff_lambda[0], e_diff_subln[0], bsz, seq,
                          q_col0, k_col0, v_col0, lambda_init)
    xf = _gmm([y_a, y_b], [e_w_out], bm=bm, bn=bn, n_cols=d, epilogue=_ep_resid,
              extras=(xf, mod), extra_specs=(_tile_spec(bm, bn), _gate_spec(bm, bn, seq, 2)),
              out_dtype=f32, name="even_out_proj")
    h = _normmod(xf, norm_gains[0, 1], mod, 3, 4, seq)
    d_ff = e_ffn_gate.shape[2]
    bnf = 256
    act = _gmm([h], [e_ffn_gate, e_ffn_up], bm=bm, bn=bnf, n_cols=d_ff, epilogue=_ep_swiglu,
               out_dtype=bf16, name="ffn_gate_up")
    half = d_ff // 2
    bmd = min(512, seq)
    part = _gmm([act], [e_ffn_down], bm=bmd, bn=bn, n_cols=d, k_block=half, k_index=0,
                epilogue=_ep_plain, out_dtype=f32, name="ffn_down_lo")
    xf = _gmm([act], [e_ffn_down], bm=bmd, bn=bn, n_cols=d, k_block=half, k_index=1,
              epilogue=_ep_resid_partial,
              extras=(xf, mod, part),
              extra_specs=(_tile_spec(bmd, bn), _gate_spec(bmd, bn, seq, 5), _tile_spec(bmd, bn)),
              out_dtype=f32, name="ffn_down_hi")

    mod = mods[1]
    dk = o_gate_w2.shape[2] // GLA_HEADS
    dv = o_gla_norm.shape[1]
    kcols = GLA_HEADS * dk
    vcols = GLA_HEADS * dv
    main_cols = 2 * kcols + 2 * vcols
    h = _normmod(xf, norm_gains[1, 0], mod, 0, 1, seq)
    proj = _gmm([h], [o_w_in], bm=bmi, bn=1024, n_cols=main_cols, epilogue=_ep_plain,
                out_dtype=bf16, name="odd_in_proj")
    log_a = _log_decay(h, o_w_in, main_cols, o_gate_w2[0], o_gate_b[0])
    o = _gla(proj, log_a, o_gla_norm[0], bsz, seq, dk, dv)
    xf = _gmm([o], [o_w_out], bm=bm, bn=bn, n_cols=d, epilogue=_ep_resid,
              extras=(xf, mod), extra_specs=(_tile_spec(bm, bn), _gate_spec(bm, bn, seq, 2)),
              out_dtype=f32, name="odd_out_proj")
    h32, top_i, top_w = _normmod_router(xf, norm_gains[1, 1], mod, 3, 4, seq, o_router[0])
    bme = min(512, seq)
    pos, row_token, tile_group, num_tiles = _routing_tables(top_i, bme)
    hs = _gather_rows(h32, row_token, min(256, bme))
    d_fe = o_exp_gate.shape[3]
    act = _gmm([hs], [o_exp_gate[0], o_exp_up[0]], bm=bme, bn=512, n_cols=d_fe,
               epilogue=_ep_swiglu, out_dtype=bf16, tile_group=tile_group, num_tiles=num_tiles,
               name="expert_gate_up")
    ye = _gmm([act], [o_exp_down[0]], bm=bme, bn=1024, n_cols=d, epilogue=_ep_slabs,
              out_dtype=f32, tile_group=tile_group, num_tiles=num_tiles, slab_out=True,
              name="expert_down")
    out = _combine(ye, pos, top_w, xf, mod, 5, final_norm, seq)
    return out.reshape(bsz, seq, d)
```

```python
import functools
import math

import jax
import jax.numpy as jnp
from jax import lax
from jax.experimental import pallas as pl
from jax.experimental.pallas import tpu as pltpu

f32 = jnp.float32
bf16 = jnp.bfloat16

N_MOD = 6
NORM_EPS = 1e-6
CONV_WIDTH = 31
CONV_HALO = 32
DIFF_HEADS = 8
DIFF_SUBLN_EPS = 1e-5
ROPE_THETA = 10000.0
GLA_HEADS = 4
GLA_RANK = 16
GLA_TAU = 16.0
GLA_CHUNK = 64
N_EXPERTS = 8
TOP_K = 2
LANES = 128
VMEM_LIMIT_BYTES = 58 * 1024 * 1024


def _cparams(n_axes):
    return pltpu.CompilerParams(
        dimension_semantics=("arbitrary",) * n_axes,
        vmem_limit_bytes=VMEM_LIMIT_BYTES)


def _sigmoid(x):
    return 1.0 / (1.0 + jnp.exp(-x))


def _silu(x):
    return x * _sigmoid(x)


def _ada_kernel(c_ref, w_ref, b_ref, o_ref):
    ca = _silu(c_ref[...]).astype(bf16)
    w = w_ref[...].astype(bf16)
    o_ref[...] = jnp.dot(ca, w, preferred_element_type=f32) + b_ref[...]


def _ada(c, ada_w, ada_b):
    depth, d, n6 = ada_w.shape
    bsz = c.shape[0]
    rows = 8
    cp = jnp.zeros((rows, d), f32).at[:bsz].set(c)
    bn = min(512, n6)
    out = pl.pallas_call(
        _ada_kernel,
        grid=(depth, n6 // bn),
        in_specs=[
            pl.BlockSpec((rows, d), lambda l, j: (0, 0)),
            pl.BlockSpec((None, d, bn), lambda l, j: (l, 0, j)),
            pl.BlockSpec((None, 1, bn), lambda l, j: (l, 0, j)),
        ],
        out_specs=pl.BlockSpec((None, rows, bn), lambda l, j: (l, 0, j)),
        out_shape=jax.ShapeDtypeStruct((depth, rows, n6), f32),
        compiler_params=_cparams(2),
        name="ada_mod",
    )(cp, ada_w, ada_b.reshape(depth, 1, n6))
    return out[:, :bsz].reshape(depth, bsz, N_MOD, 1, d)


def _normmod_value(x_ref, g_ref, sh_ref, sc_ref):
    x = x_ref[...]
    y = x * lax.rsqrt(jnp.mean(x * x, axis=-1, keepdims=True) + NORM_EPS)
    y = y * g_ref[...]
    return y * (1.0 + sc_ref[...]) + sh_ref[...]


def _normmod_kernel(x_ref, g_ref, sh_ref, sc_ref, o_ref):
    o_ref[...] = _normmod_value(x_ref, g_ref, sh_ref, sc_ref).astype(o_ref.dtype)


def _normmod_specs(bm, d, seq, i_shift, i_scale):
    return [
        pl.BlockSpec((bm, d), lambda i: (i, 0)),
        pl.BlockSpec((1, d), lambda i: (0, 0)),
        pl.BlockSpec((None, None, 1, d), lambda i: (i * bm // seq, i_shift, 0, 0)),
        pl.BlockSpec((None, None, 1, d), lambda i: (i * bm // seq, i_scale, 0, 0)),
    ]


def _normmod(x, gain, mod, i_shift, i_scale, seq):
    n, d = x.shape
    bm = min(512, seq)
    return pl.pallas_call(
        _normmod_kernel,
        grid=(n // bm,),
        in_specs=_normmod_specs(bm, d, seq, i_shift, i_scale),
        out_specs=pl.BlockSpec((bm, d), lambda i: (i, 0)),
        out_shape=jax.ShapeDtypeStruct((n, d), bf16),
        compiler_params=_cparams(1),
        name="normmod",
    )(x, gain.reshape(1, d), mod, mod)


def _store_slabs(o_ref, val):
    o_ref[...] = val.reshape(o_ref.shape)


def _normmod_router_kernel(x_ref, g_ref, sh_ref, sc_ref, wr_ref, h_ref, idx_ref, wgt_ref):
    h = _normmod_value(x_ref, g_ref, sh_ref, sc_ref)
    _store_slabs(h_ref, h)
    wr = wr_ref[...]
    h_hi = h.astype(bf16)
    h_lo = (h - h_hi.astype(f32)).astype(bf16)
    w_hi = wr.astype(bf16)
    w_lo = (wr - w_hi.astype(f32)).astype(bf16)
    logits = (jnp.dot(h_hi, w_hi, preferred_element_type=f32)
              + jnp.dot(h_hi, w_lo, preferred_element_type=f32)
              + jnp.dot(h_lo, w_hi, preferred_element_type=f32))
    col = lax.broadcasted_iota(jnp.int32, logits.shape, 1)
    neg = jnp.float32(-jnp.inf)
    l0 = jnp.where(col < N_EXPERTS, logits, neg)
    m1 = jnp.max(l0, axis=-1, keepdims=True)
    i1 = jnp.min(jnp.where(l0 == m1, col, LANES), axis=-1, keepdims=True)
    l1 = jnp.where(col == i1, neg, l0)
    m2 = jnp.max(l1, axis=-1, keepdims=True)
    i2 = jnp.min(jnp.where(l1 == m2, col, LANES), axis=-1, keepdims=True)
    e = jnp.exp(m2 - m1)
    w1 = 1.0 / (1.0 + e)
    w2 = e / (1.0 + e)
    idx_ref[...] = jnp.where(col == 0, i1, jnp.where(col == 1, i2, 0))
    wgt_ref[...] = jnp.where(col == 0, w1, jnp.where(col == 1, w2, 0.0))


def _normmod_router(x, gain, mod, i_shift, i_scale, seq, w_router):
    n, d = x.shape
    bm = min(256, seq)
    wr = jnp.zeros((d, LANES), f32).at[:, :N_EXPERTS].set(w_router)
    h, idx, wgt = pl.pallas_call(
        _normmod_router_kernel,
        grid=(n // bm,),
        in_specs=_normmod_specs(bm, d, seq, i_shift, i_scale)
        + [pl.BlockSpec((d, LANES), lambda i: (0, 0))],
        out_specs=[
            pl.BlockSpec((bm, d // LANES, LANES), lambda i: (i, 0, 0)),
            pl.BlockSpec((bm, LANES), lambda i: (i, 0)),
            pl.BlockSpec((bm, LANES), lambda i: (i, 0)),
        ],
        out_shape=[
            jax.ShapeDtypeStruct((n, d // LANES, LANES), f32),
            jax.ShapeDtypeStruct((n, LANES), jnp.int32),
            jax.ShapeDtypeStruct((n, LANES), f32),
        ],
        compiler_params=_cparams(1),
        name="normmod_router",
    )(x, gain.reshape(1, d), mod, mod, wr)
    return h, idx[:, :TOP_K], wgt


def _gmm_kernel(te_ref, nt_ref, *refs, n_x, n_w, n_extra, k_parts, epilogue, w_t):
    x_refs = refs[:n_x]
    w_refs = refs[n_x:n_x + n_w]
    e_refs = refs[n_x + n_w:n_x + n_w + n_extra]
    o_ref = refs[n_x + n_w + n_extra]
    wb_ref = refs[n_x + n_w + n_extra + 1]
    i = pl.program_id(1)
    active = i < nt_ref[0]
    prev = te_ref[jnp.maximum(i - 1, 0)]
    changed = jnp.logical_and(active, jnp.logical_or(i == 0, te_ref[i] != prev))

    @pl.when(changed)
    def _():
        for t in range(n_w):
            wb_ref[t] = w_refs[t][...].astype(bf16)

    @pl.when(active)
    def _():
        accs = []
        for t in range(n_w):
            acc = None
            off = 0
            for p in range(n_x):
                kp = k_parts[p]
                if w_t:
                    part = lax.dot_general(x_refs[p][...], wb_ref[t, :, off:off + kp],
                                           (((1,), (1,)), ((), ())), preferred_element_type=f32)
                else:
                    part = jnp.dot(x_refs[p][...], wb_ref[t, off:off + kp, :],
                                   preferred_element_type=f32)
                acc = part if acc is None else acc + part
                off += kp
            accs.append(acc)
        epilogue(accs, e_refs, o_ref)

    @pl.when(jnp.logical_not(active))
    def _():
        o_ref[...] = jnp.zeros(o_ref.shape, o_ref.dtype)


def _gmm(xs, ws, *, bm, bn, n_cols, k_block=None, k_index=0, epilogue, extras=(), extra_specs=(),
         out_dtype, tile_group=None, num_tiles=None, slab_out=False, w_t=False, name):
    m = xs[0].shape[0]
    k_parts = tuple(x.shape[1] for x in xs) if k_block is None else (k_block,)
    k = sum(k_parts)
    n_tiles = m // bm
    if tile_group is None:
        tile_group = jnp.zeros((n_tiles,), jnp.int32)
        num_tiles = jnp.full((1,), n_tiles, jnp.int32)
    xk = k_index if k_block is not None else 0

    def row(i, nt):
        return jnp.minimum(i, nt[0] - 1)

    in_specs = [pl.BlockSpec((bm, kp), lambda j, i, te, nt: (row(i, nt), xk)) for kp in k_parts]
    if w_t:
        w_block = (None, bn, k)
        w_index = lambda j, i, te, nt: (te[row(i, nt)], j, xk)
    else:
        w_block = (None, k, bn)
        w_index = lambda j, i, te, nt: (te[row(i, nt)], xk, j)
    in_specs += [pl.BlockSpec(w_block, w_index) for _ in ws]
    in_specs += list(extra_specs)
    kern = functools.partial(_gmm_kernel, n_x=len(xs), n_w=len(ws), n_extra=len(extras),
                             k_parts=k_parts, epilogue=epilogue, w_t=w_t)
    if slab_out:
        out_spec = pl.BlockSpec((bm, bn // LANES, LANES), lambda j, i, te, nt: (i, j, 0))
        out_shape = jax.ShapeDtypeStruct((m, n_cols // LANES, LANES), out_dtype)
    else:
        out_spec = pl.BlockSpec((bm, bn), lambda j, i, te, nt: (i, j))
        out_shape = jax.ShapeDtypeStruct((m, n_cols), out_dtype)
    return pl.pallas_call(
        kern,
        grid_spec=pltpu.PrefetchScalarGridSpec(
            num_scalar_prefetch=2,
            grid=(n_cols // bn, n_tiles),
            in_specs=in_specs,
            out_specs=out_spec,
            scratch_shapes=[pltpu.VMEM((len(ws),) + w_block[1:], bf16)],
        ),
        out_shape=out_shape,
        compiler_params=_cparams(2),
        name=name,
    )(tile_group, num_tiles, *xs, *ws, *extras)


def _ep_plain(accs, e_refs, o_ref):
    o_ref[...] = accs[0].astype(o_ref.dtype)


def _ep_swiglu(accs, e_refs, o_ref):
    g, u = accs
    o_ref[...] = (_silu(g) * u).astype(o_ref.dtype)


def _ep_resid(accs, e_refs, o_ref):
    res_ref, gate_ref = e_refs
    o_ref[...] = res_ref[...] + gate_ref[...] * accs[0]


def _ep_resid_partial(accs, e_refs, o_ref):
    res_ref, gate_ref, part_ref = e_refs
    o_ref[...] = res_ref[...] + gate_ref[...] * (part_ref[...] + accs[0])


def _ep_slabs(accs, e_refs, o_ref):
    acc = accs[0]
    for a in range(o_ref.shape[1]):
        o_ref[:, a, :] = acc[:, a * LANES:(a + 1) * LANES]


def _ep_rope(accs, e_refs, o_ref, *, q_tiles, k_tiles, q_scale):
    cos_ref, sin_ref = e_refs
    acc = accs[0]
    j = pl.program_id(0)
    is_q = jnp.logical_and(j >= q_tiles[0], j < q_tiles[1])
    is_k = jnp.logical_and(j >= k_tiles[0], j < k_tiles[1])

    def roped(scale):
        cos = cos_ref[...]
        sin = sin_ref[...]
        outs = []
        for hh in range(acc.shape[1] // LANES):
            a = acc[:, hh * LANES:(hh + 1) * LANES]
            rot = pltpu.roll(a, LANES // 2, 1)
            outs.append((a * cos + rot * sin) * scale)
        return jnp.concatenate(outs, axis=1)

    @pl.when(is_q)
    def _():
        o_ref[...] = roped(q_scale).astype(o_ref.dtype)

    @pl.when(is_k)
    def _():
        o_ref[...] = roped(1.0).astype(o_ref.dtype)

    @pl.when(jnp.logical_not(jnp.logical_or(is_q, is_k)))
    def _():
        o_ref[...] = acc.astype(o_ref.dtype)


def _gate_spec(bm, bn, seq, i_gate):
    return pl.BlockSpec((None, None, 1, bn),
                        lambda j, i, te, nt: (jnp.minimum(i, nt[0] - 1) * bm // seq, i_gate, 0, j))


def _tile_spec(bm, bn):
    return pl.BlockSpec((bm, bn), lambda j, i, te, nt: (jnp.minimum(i, nt[0] - 1), j))


def _conv_kernel(val_ref, gate_ref, hval_ref, hgate_ref, w_ref, b_ref, lg_ref, lb_ref,
                 o_ref, u_scr, v_scr, *, ts, seq, rows):
    i = pl.program_id(0)
    first = (i * ts) % seq == 0
    hu = hval_ref[...].astype(f32) * _sigmoid(hgate_ref[...].astype(f32))
    u_scr[0:CONV_HALO, :] = jnp.where(first, 0.0, hu)
    u_scr[CONV_HALO:, :] = val_ref[...].astype(f32) * _sigmoid(gate_ref[...].astype(f32))
    lead = CONV_HALO - (CONV_WIDTH - 1)

    cw = LANES
    sub = 8
    wrows = rows + CONV_HALO

    def chunk(c, carry):
        r0 = pl.multiple_of(c * rows, rows)
        for c0 in range(0, u_scr.shape[1], cw):
            win = u_scr[pl.ds(r0, wrows), c0:c0 + cw]
            acc = jnp.zeros((rows, cw), f32)
            for s in range(sub):
                taps = [j for j in range(CONV_WIDTH) if (lead + j) % sub == s]
                if not taps:
                    continue
                shifted = win if s == 0 else pltpu.roll(win, wrows - s, 0)
                for j in taps:
                    a = (lead + j - s)
                    acc = acc + w_ref[j:j + 1, c0:c0 + cw] * shifted[a:a + rows, :]
            v_scr[pl.ds(r0, rows), c0:c0 + cw] = acc
        return carry

    lax.fori_loop(0, ts // rows, chunk, 0)
    u = v_scr[...] + b_ref[...]
    mu = jnp.mean(u, axis=-1, keepdims=True)
    var = jnp.mean(jnp.square(u - mu), axis=-1, keepdims=True)
    y = (u - mu) * lax.rsqrt(var + NORM_EPS) * lg_ref[...] + lb_ref[...]
    o_ref[...] = _silu(y).astype(o_ref.dtype)


def _conv(proj, conv_w, conv_b, ln_g, ln_b, seq):
    n = proj.shape[0]
    ch = conv_w.shape[1]
    ts = min(256, seq)
    hb = ts // CONV_HALO
    halo_row = lambda i: jnp.maximum(i * hb - 1, 0)
    vec = lambda: pl.BlockSpec((1, ch), lambda i: (0, 0))
    kern = functools.partial(_conv_kernel, ts=ts, seq=seq, rows=64)
    return pl.pallas_call(
        kern,
        grid=(n // ts,),
        in_specs=[
            pl.BlockSpec((ts, ch), lambda i: (i, 0)),
            pl.BlockSpec((ts, ch), lambda i: (i, 1)),
            pl.BlockSpec((CONV_HALO, ch), lambda i: (halo_row(i), 0)),
            pl.BlockSpec((CONV_HALO, ch), lambda i: (halo_row(i), 1)),
            pl.BlockSpec((CONV_WIDTH, ch), lambda i: (0, 0)),
            vec(), vec(), vec(),
        ],
        out_specs=pl.BlockSpec((ts, ch), lambda i: (i, 0)),
        out_shape=jax.ShapeDtypeStruct((n, ch), bf16),
        scratch_shapes=[pltpu.VMEM((ts + CONV_HALO, ch), f32), pltpu.VMEM((ts, ch), f32)],
        compiler_params=_cparams(1),
        name="conformer_conv",
    )(proj, proj, proj, proj, conv_w, conv_b.reshape(1, ch), ln_g.reshape(1, ch), ln_b.reshape(1, ch))


def _attn_kernel(qt_ref, kt_ref, q_ref, k_ref, v_ref, lam_ref, g_ref, o_ref, m_scr, l_scr, acc_scr,
                 *, lambda_init, hd):
    t = pl.program_id(2)
    qi = qt_ref[t]
    ki = kt_ref[t]

    @pl.when(ki == 0)
    def _():
        m_scr[...] = jnp.full(m_scr.shape, -jnp.inf, f32)
        l_scr[...] = jnp.zeros(l_scr.shape, f32)
        acc_scr[...] = jnp.zeros(acc_scr.shape, f32)

    def lane_fold(x, op):
        out = x[:, 0:LANES]
        for cb in range(1, x.shape[1] // LANES):
            out = op(out, x[:, cb * LANES:(cb + 1) * LANES])
        return out

    def step(masked):
        v = v_ref[...]
        for s in range(2):
            q = q_ref[:, s * hd:(s + 1) * hd]
            k = k_ref[:, s * hd:(s + 1) * hd]
            sc = lax.dot_general(q, k, (((1,), (1,)), ((), ())), preferred_element_type=f32)
            if masked:
                row = lax.broadcasted_iota(jnp.int32, sc.shape, 0)
                col = lax.broadcasted_iota(jnp.int32, sc.shape, 1)
                sc = jnp.where(col <= row, sc, -jnp.inf)
            m_cur = jnp.max(lane_fold(sc, jnp.maximum), axis=-1, keepdims=True)
            m_prev = m_scr[s]
            m_new = jnp.maximum(m_prev, m_cur)
            alpha = jnp.exp2(m_prev - m_new)
            p = jnp.exp2(sc - jnp.tile(m_new, (1, sc.shape[1] // LANES)))
            l_scr[s] = alpha * l_scr[s] + lane_fold(p, jnp.add)
            acc_scr[s] = (jnp.tile(alpha, (1, v.shape[1] // LANES)) * acc_scr[s]
                          + jnp.dot(p.astype(bf16), v, preferred_element_type=f32))
            m_scr[s] = m_new

    @pl.when(ki < qi)
    def _():
        step(False)

    @pl.when(ki == qi)
    def _():
        step(True)
        lp = lam_ref[...]
        lam = (jnp.exp(jnp.sum(lp[0:1] * lp[1:2], axis=-1, keepdims=True))
               - jnp.exp(jnp.sum(lp[2:3] * lp[3:4], axis=-1, keepdims=True)) + lambda_init)
        l0 = jnp.sum(l_scr[0], axis=-1, keepdims=True)
        l1 = jnp.sum(l_scr[1], axis=-1, keepdims=True)
        o = acc_scr[0] / l0 - lam * (acc_scr[1] / l1)
        o = o * lax.rsqrt(jnp.mean(o * o, axis=-1, keepdims=True) + DIFF_SUBLN_EPS)
        o_ref[...] = ((o * g_ref[...]) * (1.0 - lambda_init)).astype(o_ref.dtype)


def _diff_attention(proj, lam_p, subln_g, bsz, seq, q_col0, k_col0, v_col0, lambda_init):
    n = proj.shape[0]
    hd = lam_p.shape[1]
    vd = 2 * hd
    tq = min(512, seq)
    nq = seq // tq
    pairs = [(qi, ki) for qi in range(nq) for ki in range(qi + 1)]
    q_of = jnp.asarray([p[0] for p in pairs], jnp.int32)
    k_of = jnp.asarray([p[1] for p in pairs], jnp.int32)
    kern = functools.partial(_attn_kernel, lambda_init=lambda_init, hd=hd)
    return pl.pallas_call(
        kern,
        grid_spec=pltpu.PrefetchScalarGridSpec(
            num_scalar_prefetch=2,
            grid=(bsz, DIFF_HEADS, len(pairs)),
            in_specs=[
                pl.BlockSpec((tq, vd), lambda b, h, t, qt, kt: (b * nq + qt[t], q_col0 // vd + h)),
                pl.BlockSpec((tq, vd), lambda b, h, t, qt, kt: (b * nq + kt[t], k_col0 // vd + h)),
                pl.BlockSpec((tq, vd), lambda b, h, t, qt, kt: (b * nq + kt[t], v_col0 // vd + h)),
                pl.BlockSpec((4, hd), lambda b, h, t, qt, kt: (0, 0)),
                pl.BlockSpec((1, vd), lambda b, h, t, qt, kt: (0, 0)),
            ],
            out_specs=pl.BlockSpec((tq, vd), lambda b, h, t, qt, kt: (b * nq + qt[t], h)),
            scratch_shapes=[pltpu.VMEM((2, tq, LANES), f32), pltpu.VMEM((2, tq, LANES), f32),
                            pltpu.VMEM((2, tq, vd), f32)],
        ),
        out_shape=jax.ShapeDtypeStruct((n, DIFF_HEADS * vd), bf16),
        compiler_params=_cparams(3),
        name="diff_attention",
    )(q_of, k_of, proj, proj, proj, lam_p, subln_g.reshape(1, vd))


def _loga_kernel(h_ref, w1_ref, w2_ref, b_ref, o_ref):
    rowi = lax.broadcasted_iota(jnp.int32, w1_ref.shape, 0)
    w1 = jnp.where(rowi < GLA_RANK, w1_ref[...], 0.0).astype(bf16)
    g1 = lax.dot_general(h_ref[...], w1, (((1,), (1,)), ((), ())), preferred_element_type=f32)
    gpre = jnp.dot(g1.astype(bf16), w2_ref[...].astype(bf16), preferred_element_type=f32) + b_ref[...]
    nx = -gpre
    softplus = jnp.maximum(nx, 0.0) + jnp.log1p(jnp.exp(-jnp.abs(nx)))
    o_ref[...] = -softplus / GLA_TAU


def _log_decay(h, w_in_t, col0, w2, bias):
    n, d = h.shape
    kdim = w2.shape[1]
    bm = min(512, n)
    w2p = jnp.zeros((LANES, kdim), f32).at[:GLA_RANK].set(w2)
    return pl.pallas_call(
        _loga_kernel,
        grid=(n // bm,),
        in_specs=[
            pl.BlockSpec((bm, d), lambda i: (i, 0)),
            pl.BlockSpec((None, LANES, d), lambda i: (0, col0 // LANES, 0)),
            pl.BlockSpec((LANES, kdim), lambda i: (0, 0)),
            pl.BlockSpec((1, kdim), lambda i: (0, 0)),
        ],
        out_specs=pl.BlockSpec((bm, kdim), lambda i: (i, 0)),
        out_shape=jax.ShapeDtypeStruct((n, kdim), f32),
        compiler_params=_cparams(1),
        name="gla_log_decay",
    )(h, w_in_t, w2p, bias.reshape(1, kdim))


def _gla_kernel(q_ref, k_ref, v_ref, la_ref, r_ref, g_ref, o_ref, st_ref, *, scale):
    c = pl.program_id(2)

    @pl.when(c == 0)
    def _():
        st_ref[...] = jnp.zeros(st_ref.shape, f32)

    ch = q_ref.shape[0]
    dv = v_ref.shape[1]
    la = la_ref[...]
    row = lax.broadcasted_iota(jnp.int32, (ch, ch), 0)
    col = lax.broadcasted_iota(jnp.int32, (ch, ch), 1)
    causal = row >= col
    hi = lax.Precision.HIGHEST
    b = jnp.dot(causal.astype(f32), la, preferred_element_type=f32, precision=hi)
    b_last = b[ch - 1:ch, :]
    b_mid = b[ch // 2 - 1:ch // 2, :]
    b_last_col = lax.dot_general(la, jnp.ones((ch, LANES), f32), (((0,), (0,)), ((), ())),
                                 preferred_element_type=f32, precision=hi)
    q = q_ref[...].astype(f32) * scale
    k = k_ref[...].astype(f32)
    q_t = (q * jnp.exp(b)).astype(bf16)
    q_rel = (q * jnp.exp(b - b_mid)).astype(bf16)
    k_rel = (k * jnp.exp(b_mid - b)).astype(bf16)
    k_dec = (k * jnp.exp(b_last - b)).astype(bf16)
    v = v_ref[...]
    attn = lax.dot_general(q_rel, k_rel, (((1,), (1,)), ((), ())), preferred_element_type=f32)
    attn = jnp.where(causal, attn, 0.0)
    st = st_ref[...]
    o = (jnp.dot(attn.astype(bf16), v, preferred_element_type=f32)
         + jnp.dot(q_t, st.astype(bf16), preferred_element_type=f32))
    kv = lax.dot_general(k_dec, v, (((0,), (0,)), ((), ())), preferred_element_type=f32)
    st_ref[...] = jnp.tile(jnp.exp(b_last_col), (1, dv // LANES)) * st + kv
    o = o * lax.rsqrt(jnp.mean(o * o, axis=-1, keepdims=True) + NORM_EPS) * g_ref[...]
    o_ref[...] = (o * _silu(r_ref[...].astype(f32))).astype(o_ref.dtype)


def _gla(proj, log_a, norm_g, bsz, seq, dk, dv):
    n = proj.shape[0]
    ch = 2 * GLA_CHUNK if seq % (2 * GLA_CHUNK) == 0 else GLA_CHUNK
    nc = seq // ch
    kcols = GLA_HEADS * dk
    vcols = GLA_HEADS * dv
    rowb = lambda b, h, c: b * nc + c
    kern = functools.partial(_gla_kernel, scale=dk ** -0.5)
    return pl.pallas_call(
        kern,
        grid=(bsz, GLA_HEADS, nc),
        in_specs=[
            pl.BlockSpec((ch, dk), lambda b, h, c: (rowb(b, h, c), h)),
            pl.BlockSpec((ch, dk), lambda b, h, c: (rowb(b, h, c), kcols // dk + h)),
            pl.BlockSpec((ch, dv), lambda b, h, c: (rowb(b, h, c), 2 * kcols // dv + h)),
            pl.BlockSpec((ch, dk), lambda b, h, c: (rowb(b, h, c), h)),
            pl.BlockSpec((ch, dv), lambda b, h, c: (rowb(b, h, c), (2 * kcols + vcols) // dv + h)),
            pl.BlockSpec((1, dv), lambda b, h, c: (0, 0)),
        ],
        out_specs=pl.BlockSpec((ch, dv), lambda b, h, c: (rowb(b, h, c), h)),
        out_shape=jax.ShapeDtypeStruct((n, vcols), bf16),
        scratch_shapes=[pltpu.VMEM((dk, dv), f32)],
        compiler_params=_cparams(3),
        name="gla_chunked",
    )(proj, proj, proj, log_a, proj, norm_g.reshape(1, dv))


GATHER_UNROLL = 8


def _slab_copy(src_hbm, dst_vmem, sem, src_row, dst_row):
    return pltpu.make_async_copy(src_hbm.at[pl.ds(src_row, 1)], dst_vmem.at[pl.ds(dst_row, 1)], sem)


def _issue_slabs(idx_ref, n_idx, src_hbm, dst, sem):
    def body(r2, carry):
        for u in range(2):
            r = 2 * r2 + u
            _slab_copy(src_hbm, dst, sem, idx_ref[0, r], r).start(priority=u)
        return carry

    lax.fori_loop(0, n_idx // 2, body, 0, unroll=GATHER_UNROLL // 2)


def _wait_slabs(n_idx, src_hbm, dst, sem):
    def body(r, carry):
        _slab_copy(src_hbm, dst, sem, 0, r).wait()
        return carry

    lax.fori_loop(0, n_idx, body, 0, unroll=GATHER_UNROLL)


def _fetch_slabs(cur_ref, nxt_ref, src_hbm, buf, sem, n_idx):
    s = pl.program_id(0)
    slot = s % 2

    @pl.when(s == 0)
    def _():
        _issue_slabs(cur_ref, n_idx, src_hbm, buf.at[0], sem.at[0])

    @pl.when(s + 1 < pl.num_programs(0))
    def _():
        _issue_slabs(nxt_ref, n_idx, src_hbm, buf.at[1 - slot], sem.at[1 - slot])

    _wait_slabs(n_idx, src_hbm, buf.at[slot], sem.at[slot])
    return slot


def _idx_specs(n_steps, n_idx):
    return [
        pl.BlockSpec((None, 1, n_idx), lambda i: (i, 0, 0), memory_space=pltpu.SMEM),
        pl.BlockSpec((None, 1, n_idx), lambda i: (jnp.minimum(i + 1, n_steps - 1), 0, 0),
                     memory_space=pltpu.SMEM),
    ]


def _gather_kernel(cur_ref, nxt_ref, h_hbm, o_ref, buf, sem, *, tr):
    slot = _fetch_slabs(cur_ref, nxt_ref, h_hbm, buf, sem, tr)
    o_ref[...] = buf[slot].reshape(o_ref.shape).astype(o_ref.dtype)


def _gather_rows(h, row_token, tr):
    n, nslab, _ = h.shape
    r = row_token.shape[0]
    kern = functools.partial(_gather_kernel, tr=tr)
    idx = row_token.reshape(r // tr, 1, tr)
    return pl.pallas_call(
        kern,
        grid=(r // tr,),
        in_specs=_idx_specs(r // tr, tr) + [pl.BlockSpec(memory_space=pl.ANY)],
        out_specs=pl.BlockSpec((tr, nslab * LANES), lambda i: (i, 0)),
        out_shape=jax.ShapeDtypeStruct((r, nslab * LANES), bf16),
        scratch_shapes=[pltpu.VMEM((2, tr, nslab, LANES), f32), pltpu.SemaphoreType.DMA((2,))],
        compiler_params=_cparams(1),
        name="moe_gather",
    )(idx, idx, h)


def _combine_kernel(cur_ref, nxt_ref, y_hbm, w_ref, x_ref, gate_ref, fn_ref, o_ref, buf, sem, *, tt):
    slot = _fetch_slabs(cur_ref, nxt_ref, y_hbm, buf, sem, TOP_K * tt)
    ys = buf[slot].reshape(TOP_K * tt, x_ref.shape[1])
    w = w_ref[...]
    moe = w[:, 0:1] * ys[0:tt] + w[:, 1:2] * ys[tt:2 * tt]
    x = x_ref[...] + gate_ref[...] * moe
    y = x * lax.rsqrt(jnp.mean(x * x, axis=-1, keepdims=True) + NORM_EPS)
    o_ref[...] = y * fn_ref[...]


def _combine(y, pos, top_w, x, mod, i_gate, final_norm, seq):
    n, d = x.shape
    nslab = d // LANES
    tt = min(128, seq)
    kern = functools.partial(_combine_kernel, tt=tt)
    idx = pos.reshape(n // tt, tt, TOP_K).transpose(0, 2, 1).reshape(n // tt, 1, TOP_K * tt)
    return pl.pallas_call(
        kern,
        grid=(n // tt,),
        in_specs=_idx_specs(n // tt, TOP_K * tt) + [
            pl.BlockSpec(memory_space=pl.ANY),
            pl.BlockSpec((tt, LANES), lambda i: (i, 0)),
            pl.BlockSpec((tt, d), lambda i: (i, 0)),
            pl.BlockSpec((None, None, 1, d), lambda i: (i * tt // seq, i_gate, 0, 0)),
            pl.BlockSpec((1, d), lambda i: (0, 0)),
        ],
        out_specs=pl.BlockSpec((tt, d), lambda i: (i, 0)),
        out_shape=jax.ShapeDtypeStruct((n, d), f32),
        scratch_shapes=[pltpu.VMEM((2, TOP_K * tt, nslab, LANES), f32),
                        pltpu.SemaphoreType.DMA((2,))],
        compiler_params=_cparams(1),
        name="moe_combine_final_norm",
    )(idx, idx, y, top_w, x, mod, final_norm.reshape(1, d))


def _routing_tables(top_i, bm):
    n = top_i.shape[0]
    e_flat = top_i.reshape(-1)
    onehot = (e_flat[:, None] == jnp.arange(N_EXPERTS)[None, :]).astype(jnp.int32)
    counts = jnp.sum(onehot, axis=0)
    rank = jnp.take_along_axis(jnp.cumsum(onehot, axis=0) - onehot, e_flat[:, None], axis=1)[:, 0]
    padded = ((counts + bm - 1) // bm) * bm
    ends = jnp.cumsum(padded)
    starts = ends - padded
    pos = starts[e_flat] + rank
    n_rows = n * TOP_K + N_EXPERTS * bm
    n_tiles = n_rows // bm
    num_tiles = (ends[-1] // bm).astype(jnp.int32)
    tile_start = jnp.minimum(jnp.arange(n_tiles), num_tiles - 1) * bm
    tile_group = jnp.minimum(jnp.searchsorted(ends, tile_start, side="right"),
                             N_EXPERTS - 1).astype(jnp.int32)
    row_token = jnp.zeros((n_rows,), jnp.int32).at[pos].set(jnp.arange(n * TOP_K, dtype=jnp.int32) // TOP_K)
    return pos.astype(jnp.int32), row_token, tile_group, num_tiles.reshape(1)


def _rope_tables(seq, hd):
    inv = ROPE_THETA ** (-jnp.arange(0, hd, 2, dtype=f32) / hd)
    ang = jnp.arange(seq).astype(f32)[:, None] * inv[None, :]
    cos, sin = jnp.cos(ang), jnp.sin(ang)
    return jnp.concatenate([cos, cos], axis=-1), jnp.concatenate([-sin, sin], axis=-1)


def kernel(x, c, norm_gains, ada_w, ada_b, e_w_in, e_conv_w, e_conv_b, e_conv_ln_g, e_conv_ln_b, e_diff_lambda, e_diff_subln, e_w_out, e_ffn_gate, e_ffn_up, e_ffn_down, o_w_in, o_gate_w2, o_gate_b, o_gla_norm, o_w_out, o_router, o_exp_gate, o_exp_up, o_exp_down, final_norm):
    bsz, seq, d = x.shape
    n = bsz * seq
    depth = ada_w.shape[0]
    assert depth == 2, "trunk is one even (conv + diff-attn) and one odd (GLA + experts) layer"
    xf = x.reshape(n, d)
    mods = _ada(c, ada_w, ada_b)
    bm = min(1024, seq)

    mod = mods[0]
    conv_ch = e_conv_w.shape[2]
    hd = e_diff_lambda.shape[2]
    qk_cols = 2 * DIFF_HEADS * hd
    q_col0 = 2 * conv_ch
    k_col0 = q_col0 + qk_cols
    v_col0 = k_col0 + qk_cols
    even_in = e_w_in.shape[2]
    bn = 512
    h = _normmod(xf, norm_gains[0, 0], mod, 0, 1, seq)
    cos_t, sin_t = _rope_tables(seq, hd)
    bmi = min(512, seq)
    bni = max(t for t in (1024, 512, 256)
              if all(cc % t == 0 for cc in (q_col0, k_col0, v_col0, even_in)))
    rope_spec = pl.BlockSpec(
        (bmi, hd), lambda j, i, te, nt: (jnp.minimum(i, nt[0] - 1) % (seq // bmi), 0))
    ep = functools.partial(_ep_rope, q_tiles=(q_col0 // bni, k_col0 // bni),
                           k_tiles=(k_col0 // bni, v_col0 // bni),
                           q_scale=hd ** -0.5 * math.log2(math.e))
    proj = _gmm([h], [e_w_in], bm=bmi, bn=bni, n_cols=even_in, epilogue=ep,
                extras=(cos_t, sin_t), extra_specs=(rope_spec, rope_spec),
                out_dtype=bf16, name="even_in_proj")
    y_a = _conv(proj, e_conv_w[0], e_conv_b[0], e_conv_ln_g[0], e_conv_ln_b[0], seq)
    lambda_init = 0.8 - 0.6 * math.exp(-0.3 * 0)
    y_b = _diff_attention(proj, e_diff_lambda[0], e_diff_subln[0], bsz, seq,
                          q_col0, k_col0, v_col0, lambda_init)
    xf = _gmm([y_a, y_b], [e_w_out], bm=bm, bn=bn, n_cols=d, epilogue=_ep_resid,
              extras=(xf, mod), extra_specs=(_tile_spec(bm, bn), _gate_spec(bm, bn, seq, 2)),
              out_dtype=f32, name="even_out_proj")
    h = _normmod(xf, norm_gains[0, 1], mod, 3, 4, seq)
    d_ff = e_ffn_gate.shape[2]
    bnf = 256
    act = _gmm([h], [e_ffn_gate, e_ffn_up], bm=bm, bn=bnf, n_cols=d_ff, epilogue=_ep_swiglu,
               out_dtype=bf16, name="ffn_gate_up")
    half = d_ff // 2
    bmd = min(512, seq)
    part = _gmm([act], [e_ffn_down], bm=bmd, bn=bn, n_cols=d, k_block=half, k_index=0,
                epilogue=_ep_plain, out_dtype=f32, name="ffn_down_lo")
    xf = _gmm([act], [e_ffn_down], bm=bmd, bn=bn, n_cols=d, k_block=half, k_index=1,
              epilogue=_ep_resid_partial,
              extras=(xf, mod, part),
              extra_specs=(_tile_spec(bmd, bn), _gate_spec(bmd, bn, seq, 5), _tile_spec(bmd, bn)),
              out_dtype=f32, name="ffn_down_hi")

    mod = mods[1]
    dk = o_gate_w2.shape[2] // GLA_HEADS
    dv = o_gla_norm.shape[1]
    kcols = GLA_HEADS * dk
    vcols = GLA_HEADS * dv
    main_cols = 2 * kcols + 2 * vcols
    h = _normmod(xf, norm_gains[1, 0], mod, 0, 1, seq)
    w_in_t = jnp.swapaxes(o_w_in, 1, 2)
    proj = _gmm([h], [w_in_t], bm=bmi, bn=1024, n_cols=main_cols, epilogue=_ep_plain,
                out_dtype=bf16, w_t=True, name="odd_in_proj")
    log_a = _log_decay(h, w_in_t, main_cols, o_gate_w2[0], o_gate_b[0])
    o = _gla(proj, log_a, o_gla_norm[0], bsz, seq, dk, dv)
    xf = _gmm([o], [o_w_out], bm=bm, bn=bn, n_cols=d, epilogue=_ep_resid,
              extras=(xf, mod), extra_specs=(_tile_spec(bm, bn), _gate_spec(bm, bn, seq, 2)),
              out_dtype=f32, name="odd_out_proj")
    h32, top_i, top_w = _normmod_router(xf, norm_gains[1, 1], mod, 3, 4, seq, o_router[0])
    bme = min(512, seq)
    pos, row_token, tile_group, num_tiles = _routing_tables(top_i, bme)
    hs = _gather_rows(h32, row_token, min(256, bme))
    d_fe = o_exp_gate.shape[3]
    act = _gmm([hs], [o_exp_gate[0], o_exp_up[0]], bm=bme, bn=512, n_cols=d_fe,
               epilogue=_ep_swiglu, out_dtype=bf16, tile_group=tile_group, num_tiles=num_tiles,
               name="expert_gate_up")
    ye = _gmm([act], [o_exp_down[0]], bm=bme, bn=1024, n_cols=d, epilogue=_ep_slabs,
              out_dtype=f32, tile_group=tile_group, num_tiles=num_tiles, slab_out=True,
              name="expert_down")
    out = _combine(ye, pos, top_w, xf, mod, 5, final_norm, seq)
    return out.reshape(bsz, seq, d)
```

```python
import functools
import math

import jax
import jax.numpy as jnp
from jax import lax
from jax.experimental import pallas as pl
from jax.experimental.pallas import tpu as pltpu

f32 = jnp.float32
bf16 = jnp.bfloat16

N_MOD = 6
NORM_EPS = 1e-6
CONV_WIDTH = 31
CONV_HALO = 32
DIFF_HEADS = 8
DIFF_SUBLN_EPS = 1e-5
ROPE_THETA = 10000.0
GLA_HEADS = 4
GLA_RANK = 16
GLA_TAU = 16.0
GLA_CHUNK = 64
N_EXPERTS = 8
TOP_K = 2
LANES = 128
VMEM_LIMIT_BYTES = 58 * 1024 * 1024


def _cparams(n_axes):
    return pltpu.CompilerParams(
        dimension_semantics=("arbitrary",) * n_axes,
        vmem_limit_bytes=VMEM_LIMIT_BYTES)


def _sigmoid(x):
    return 1.0 / (1.0 + jnp.exp(-x))


def _silu(x):
    return x * _sigmoid(x)


def _ada_kernel(c_ref, w_ref, b_ref, o_ref):
    ca = _silu(c_ref[...]).astype(bf16)
    w = w_ref[...].astype(bf16)
    o_ref[...] = jnp.dot(ca, w, preferred_element_type=f32) + b_ref[...]


def _ada(c, ada_w, ada_b):
    depth, d, n6 = ada_w.shape
    bsz = c.shape[0]
    rows = 8
    cp = jnp.zeros((rows, d), f32).at[:bsz].set(c)
    bn = min(512, n6)
    out = pl.pallas_call(
        _ada_kernel,
        grid=(depth, n6 // bn),
        in_specs=[
            pl.BlockSpec((rows, d), lambda l, j: (0, 0)),
            pl.BlockSpec((None, d, bn), lambda l, j: (l, 0, j)),
            pl.BlockSpec((None, 1, bn), lambda l, j: (l, 0, j)),
        ],
        out_specs=pl.BlockSpec((None, rows, bn), lambda l, j: (l, 0, j)),
        out_shape=jax.ShapeDtypeStruct((depth, rows, n6), f32),
        compiler_params=_cparams(2),
        name="ada_mod",
    )(cp, ada_w, ada_b.reshape(depth, 1, n6))
    return out[:, :bsz].reshape(depth, bsz, N_MOD, 1, d)


def _normmod_value(x_ref, g_ref, sh_ref, sc_ref):
    x = x_ref[...]
    y = x * lax.rsqrt(jnp.mean(x * x, axis=-1, keepdims=True) + NORM_EPS)
    y = y * g_ref[...]
    return y * (1.0 + sc_ref[...]) + sh_ref[...]


def _normmod_kernel(x_ref, g_ref, sh_ref, sc_ref, o_ref):
    o_ref[...] = _normmod_value(x_ref, g_ref, sh_ref, sc_ref).astype(o_ref.dtype)


def _normmod_specs(bm, d, seq, i_shift, i_scale):
    return [
        pl.BlockSpec((bm, d), lambda i: (i, 0)),
        pl.BlockSpec((1, d), lambda i: (0, 0)),
        pl.BlockSpec((None, None, 1, d), lambda i: (i * bm // seq, i_shift, 0, 0)),
        pl.BlockSpec((None, None, 1, d), lambda i: (i * bm // seq, i_scale, 0, 0)),
    ]


def _normmod(x, gain, mod, i_shift, i_scale, seq):
    n, d = x.shape
    bm = min(512, seq)
    return pl.pallas_call(
        _normmod_kernel,
        grid=(n // bm,),
        in_specs=_normmod_specs(bm, d, seq, i_shift, i_scale),
        out_specs=pl.BlockSpec((bm, d), lambda i: (i, 0)),
        out_shape=jax.ShapeDtypeStruct((n, d), bf16),
        compiler_params=_cparams(1),
        name="normmod",
    )(x, gain.reshape(1, d), mod, mod)


def _store_slabs(o_ref, val):
    o_ref[...] = val.reshape(o_ref.shape)


def _normmod_router_kernel(x_ref, g_ref, sh_ref, sc_ref, wr_ref, h_ref, idx_ref, wgt_ref):
    h = _normmod_value(x_ref, g_ref, sh_ref, sc_ref)
    _store_slabs(h_ref, h)
    wr = wr_ref[...]
    h_hi = h.astype(bf16)
    h_lo = (h - h_hi.astype(f32)).astype(bf16)
    w_hi = wr.astype(bf16)
    w_lo = (wr - w_hi.astype(f32)).astype(bf16)
    logits = (jnp.dot(h_hi, w_hi, preferred_element_type=f32)
              + jnp.dot(h_hi, w_lo, preferred_element_type=f32)
              + jnp.dot(h_lo, w_hi, preferred_element_type=f32))
    col = lax.broadcasted_iota(jnp.int32, logits.shape, 1)
    neg = jnp.float32(-jnp.inf)
    l0 = jnp.where(col < N_EXPERTS, logits, neg)
    m1 = jnp.max(l0, axis=-1, keepdims=True)
    i1 = jnp.min(jnp.where(l0 == m1, col, LANES), axis=-1, keepdims=True)
    l1 = jnp.where(col == i1, neg, l0)
    m2 = jnp.max(l1, axis=-1, keepdims=True)
    i2 = jnp.min(jnp.where(l1 == m2, col, LANES), axis=-1, keepdims=True)
    e = jnp.exp(m2 - m1)
    w1 = 1.0 / (1.0 + e)
    w2 = e / (1.0 + e)
    idx_ref[...] = jnp.where(col == 0, i1, jnp.where(col == 1, i2, 0))
    wgt_ref[...] = jnp.where(col == 0, w1, jnp.where(col == 1, w2, 0.0))


def _normmod_router(x, gain, mod, i_shift, i_scale, seq, w_router):
    n, d = x.shape
    bm = min(256, seq)
    wr = jnp.zeros((d, LANES), f32).at[:, :N_EXPERTS].set(w_router)
    h, idx, wgt = pl.pallas_call(
        _normmod_router_kernel,
        grid=(n // bm,),
        in_specs=_normmod_specs(bm, d, seq, i_shift, i_scale)
        + [pl.BlockSpec((d, LANES), lambda i: (0, 0))],
        out_specs=[
            pl.BlockSpec((bm, d // LANES, LANES), lambda i: (i, 0, 0)),
            pl.BlockSpec((bm, LANES), lambda i: (i, 0)),
            pl.BlockSpec((bm, LANES), lambda i: (i, 0)),
        ],
        out_shape=[
            jax.ShapeDtypeStruct((n, d // LANES, LANES), f32),
            jax.ShapeDtypeStruct((n, LANES), jnp.int32),
            jax.ShapeDtypeStruct((n, LANES), f32),
        ],
        compiler_params=_cparams(1),
        name="normmod_router",
    )(x, gain.reshape(1, d), mod, mod, wr)
    return h, idx[:, :TOP_K], wgt


def _gmm_kernel(te_ref, nt_ref, *refs, n_x, n_w, n_extra, k_parts, epilogue, w_t, n_tiles, bn, k_row0):
    x_refs = refs[:n_x]
    w_hbm = refs[n_x:n_x + n_w]
    e_refs = refs[n_x + n_w:n_x + n_w + n_extra]
    o_ref = refs[n_x + n_w + n_extra]
    wf_ref, wb_ref, sem = refs[n_x + n_w + n_extra + 1:]
    j = pl.program_id(0)
    i = pl.program_id(1)
    active = i < nt_ref[0]
    prev = te_ref[jnp.maximum(i - 1, 0)]
    changed = jnp.logical_and(active, jnp.logical_or(i == 0, te_ref[i] != prev))
    k = sum(k_parts)

    def w_copy(t, g, jj, slot):
        c0 = pl.multiple_of(jj * bn, bn)
        if w_t:
            src = w_hbm[t].at[g, pl.ds(c0, bn), pl.ds(k_row0, k)]
        else:
            src = w_hbm[t].at[g, pl.ds(k_row0, k), pl.ds(c0, bn)]
        return pltpu.make_async_copy(src, wf_ref.at[slot, t], sem.at[slot])

    @pl.when(changed)
    def _():
        run = j * nt_ref[1] + te_ref[2 * n_tiles + i]
        slot = run % 2
        g = te_ref[i]

        @pl.when(run == 0)
        def _():
            for t in range(n_w):
                w_copy(t, g, j, slot).start()

        for t in range(n_w):
            w_copy(t, g, j, slot).wait()
        nxt = te_ref[n_tiles + i]
        more_groups = nxt >= 0
        g2 = jnp.where(more_groups, te_ref[jnp.maximum(nxt, 0)], te_ref[0])
        j2 = jnp.where(more_groups, j, j + 1)

        @pl.when(j2 < pl.num_programs(0))
        def _():
            for t in range(n_w):
                w_copy(t, g2, j2, 1 - slot).start()

        for t in range(n_w):
            wb_ref[t] = wf_ref[slot, t].astype(bf16)

    @pl.when(active)
    def _():
        accs = []
        for t in range(n_w):
            acc = None
            off = 0
            for p in range(n_x):
                kp = k_parts[p]
                if w_t:
                    part = lax.dot_general(x_refs[p][...], wb_ref[t, :, off:off + kp],
                                           (((1,), (1,)), ((), ())), preferred_element_type=f32)
                else:
                    part = jnp.dot(x_refs[p][...], wb_ref[t, off:off + kp, :],
                                   preferred_element_type=f32)
                acc = part if acc is None else acc + part
                off += kp
            accs.append(acc)
        epilogue(accs, e_refs, o_ref)

    @pl.when(jnp.logical_not(active))
    def _():
        o_ref[...] = jnp.zeros(o_ref.shape, o_ref.dtype)


def _gmm(xs, ws, *, bm, bn, n_cols, k_block=None, k_index=0, epilogue, extras=(), extra_specs=(),
         out_dtype, tile_group=None, num_tiles=None, slab_out=False, w_t=False, name):
    m = xs[0].shape[0]
    k_parts = tuple(x.shape[1] for x in xs) if k_block is None else (k_block,)
    k = sum(k_parts)
    n_tiles = m // bm
    if tile_group is None:
        tile_group = jnp.zeros((n_tiles,), jnp.int32)
        num_tiles = jnp.full((1,), n_tiles, jnp.int32)
    xk = k_index if k_block is not None else 0
    tid = jnp.arange(n_tiles, dtype=jnp.int32)
    first = jnp.logical_and(tid < num_tiles[0],
                            jnp.concatenate([jnp.ones((1,), bool), tile_group[1:] != tile_group[:-1]]))
    ordinal = jnp.cumsum(first.astype(jnp.int32)) - 1
    first_at = jnp.where(first, tid, n_tiles)
    nxt = jnp.concatenate([lax.cummin(first_at[::-1])[::-1][1:], jnp.full((1,), n_tiles, jnp.int32)])
    nxt = jnp.where(nxt >= n_tiles, -1, nxt).astype(jnp.int32)
    tables = jnp.concatenate([tile_group.astype(jnp.int32), nxt, ordinal.astype(jnp.int32)])
    counts = jnp.stack([num_tiles[0], ordinal[jnp.maximum(num_tiles[0] - 1, 0)] + 1]).astype(jnp.int32)

    def row(i, nt):
        return jnp.minimum(i, nt[0] - 1)

    in_specs = [pl.BlockSpec((bm, kp), lambda j, i, te, nt: (row(i, nt), xk)) for kp in k_parts]
    in_specs += [pl.BlockSpec(memory_space=pl.ANY) for _ in ws]
    in_specs += list(extra_specs)
    w_tile = (bn, k) if w_t else (k, bn)
    kern = functools.partial(_gmm_kernel, n_x=len(xs), n_w=len(ws), n_extra=len(extras),
                             k_parts=k_parts, epilogue=epilogue, w_t=w_t, n_tiles=n_tiles, bn=bn,
                             k_row0=xk * k)
    if slab_out:
        out_spec = pl.BlockSpec((bm, bn // LANES, LANES), lambda j, i, te, nt: (i, j, 0))
        out_shape = jax.ShapeDtypeStruct((m, n_cols // LANES, LANES), out_dtype)
    else:
        out_spec = pl.BlockSpec((bm, bn), lambda j, i, te, nt: (i, j))
        out_shape = jax.ShapeDtypeStruct((m, n_cols), out_dtype)
    return pl.pallas_call(
        kern,
        grid_spec=pltpu.PrefetchScalarGridSpec(
            num_scalar_prefetch=2,
            grid=(n_cols // bn, n_tiles),
            in_specs=in_specs,
            out_specs=out_spec,
            scratch_shapes=[pltpu.VMEM((2, len(ws)) + w_tile, f32),
                            pltpu.VMEM((len(ws),) + w_tile, bf16),
                            pltpu.SemaphoreType.DMA((2,))],
        ),
        out_shape=out_shape,
        compiler_params=_cparams(2),
        name=name,
    )(tables, counts, *xs, *ws, *extras)


def _ep_plain(accs, e_refs, o_ref):
    o_ref[...] = accs[0].astype(o_ref.dtype)


def _ep_swiglu(accs, e_refs, o_ref):
    g, u = accs
    o_ref[...] = (_silu(g) * u).astype(o_ref.dtype)


def _ep_resid(accs, e_refs, o_ref):
    res_ref, gate_ref = e_refs
    o_ref[...] = res_ref[...] + gate_ref[...] * accs[0]


def _ep_resid_partial(accs, e_refs, o_ref):
    res_ref, gate_ref, part_ref = e_refs
    o_ref[...] = res_ref[...] + gate_ref[...] * (part_ref[...] + accs[0])


def _ep_slabs(accs, e_refs, o_ref):
    _store_slabs(o_ref, accs[0])


def _ep_rope(accs, e_refs, o_ref, *, q_tiles, k_tiles, q_scale):
    cos_ref, sin_ref = e_refs
    acc = accs[0]
    j = pl.program_id(0)
    is_q = jnp.logical_and(j >= q_tiles[0], j < q_tiles[1])
    is_k = jnp.logical_and(j >= k_tiles[0], j < k_tiles[1])

    def roped(scale):
        cos = cos_ref[...]
        sin = sin_ref[...]
        outs = []
        for hh in range(acc.shape[1] // LANES):
            a = acc[:, hh * LANES:(hh + 1) * LANES]
            rot = pltpu.roll(a, LANES // 2, 1)
            outs.append((a * cos + rot * sin) * scale)
        return jnp.concatenate(outs, axis=1)

    @pl.when(is_q)
    def _():
        o_ref[...] = roped(q_scale).astype(o_ref.dtype)

    @pl.when(is_k)
    def _():
        o_ref[...] = roped(1.0).astype(o_ref.dtype)

    @pl.when(jnp.logical_not(jnp.logical_or(is_q, is_k)))
    def _():
        o_ref[...] = acc.astype(o_ref.dtype)


def _gate_spec(bm, bn, seq, i_gate):
    return pl.BlockSpec((None, None, 1, bn),
                        lambda j, i, te, nt: (jnp.minimum(i, nt[0] - 1) * bm // seq, i_gate, 0, j))


def _tile_spec(bm, bn):
    return pl.BlockSpec((bm, bn), lambda j, i, te, nt: (jnp.minimum(i, nt[0] - 1), j))


def _conv_kernel(val_ref, gate_ref, hval_ref, hgate_ref, w_ref, b_ref, lg_ref, lb_ref,
                 o_ref, u_scr, v_scr, *, ts, seq, rows):
    i = pl.program_id(0)
    first = (i * ts) % seq == 0
    hu = hval_ref[...].astype(f32) * _sigmoid(hgate_ref[...].astype(f32))
    u_scr[0:CONV_HALO, :] = jnp.where(first, 0.0, hu)
    u_scr[CONV_HALO:, :] = val_ref[...].astype(f32) * _sigmoid(gate_ref[...].astype(f32))
    lead = CONV_HALO - (CONV_WIDTH - 1)

    cw = LANES
    sub = 8
    wrows = rows + CONV_HALO

    def chunk(c, carry):
        r0 = pl.multiple_of(c * rows, rows)
        for c0 in range(0, u_scr.shape[1], cw):
            win = u_scr[pl.ds(r0, wrows), c0:c0 + cw]
            acc = jnp.zeros((rows, cw), f32)
            for s in range(sub):
                taps = [j for j in range(CONV_WIDTH) if (lead + j) % sub == s]
                if not taps:
                    continue
                shifted = win if s == 0 else pltpu.roll(win, wrows - s, 0)
                for j in taps:
                    a = (lead + j - s)
                    acc = acc + w_ref[j:j + 1, c0:c0 + cw] * shifted[a:a + rows, :]
            v_scr[pl.ds(r0, rows), c0:c0 + cw] = acc
        return carry

    lax.fori_loop(0, ts // rows, chunk, 0)
    u = v_scr[...] + b_ref[...]
    mu = jnp.mean(u, axis=-1, keepdims=True)
    var = jnp.mean(jnp.square(u - mu), axis=-1, keepdims=True)
    y = (u - mu) * lax.rsqrt(var + NORM_EPS) * lg_ref[...] + lb_ref[...]
    o_ref[...] = _silu(y).astype(o_ref.dtype)


def _conv(proj, conv_w, conv_b, ln_g, ln_b, seq):
    n = proj.shape[0]
    ch = conv_w.shape[1]
    ts = min(256, seq)
    hb = ts // CONV_HALO
    halo_row = lambda i: jnp.maximum(i * hb - 1, 0)
    vec = lambda: pl.BlockSpec((1, ch), lambda i: (0, 0))
    kern = functools.partial(_conv_kernel, ts=ts, seq=seq, rows=64)
    return pl.pallas_call(
        kern,
        grid=(n // ts,),
        in_specs=[
            pl.BlockSpec((ts, ch), lambda i: (i, 0)),
            pl.BlockSpec((ts, ch), lambda i: (i, 1)),
            pl.BlockSpec((CONV_HALO, ch), lambda i: (halo_row(i), 0)),
            pl.BlockSpec((CONV_HALO, ch), lambda i: (halo_row(i), 1)),
            pl.BlockSpec((CONV_WIDTH, ch), lambda i: (0, 0)),
            vec(), vec(), vec(),
        ],
        out_specs=pl.BlockSpec((ts, ch), lambda i: (i, 0)),
        out_shape=jax.ShapeDtypeStruct((n, ch), bf16),
        scratch_shapes=[pltpu.VMEM((ts + CONV_HALO, ch), f32), pltpu.VMEM((ts, ch), f32)],
        compiler_params=_cparams(1),
        name="conformer_conv",
    )(proj, proj, proj, proj, conv_w, conv_b.reshape(1, ch), ln_g.reshape(1, ch), ln_b.reshape(1, ch))


def _attn_kernel(qt_ref, kt_ref, q_ref, k_ref, v_ref, lam_ref, g_ref, o_ref, m_scr, l_scr, acc_scr,
                 *, lambda_init, hd):
    t = pl.program_id(2)
    qi = qt_ref[t]
    ki = kt_ref[t]

    @pl.when(ki == 0)
    def _():
        m_scr[...] = jnp.full(m_scr.shape, -jnp.inf, f32)
        l_scr[...] = jnp.zeros(l_scr.shape, f32)
        acc_scr[...] = jnp.zeros(acc_scr.shape, f32)

    def lane_fold(x, op):
        out = x[:, 0:LANES]
        for cb in range(1, x.shape[1] // LANES):
            out = op(out, x[:, cb * LANES:(cb + 1) * LANES])
        return out

    def step(masked):
        v = v_ref[...]
        for s in range(2):
            q = q_ref[:, s * hd:(s + 1) * hd]
            k = k_ref[:, s * hd:(s + 1) * hd]
            sc = lax.dot_general(q, k, (((1,), (1,)), ((), ())), preferred_element_type=f32)
            if masked:
                row = lax.broadcasted_iota(jnp.int32, sc.shape, 0)
                col = lax.broadcasted_iota(jnp.int32, sc.shape, 1)
                sc = jnp.where(col <= row, sc, -jnp.inf)
            m_cur = jnp.max(lane_fold(sc, jnp.maximum), axis=-1, keepdims=True)
            m_prev = m_scr[s]
            m_new = jnp.maximum(m_prev, m_cur)
            alpha = jnp.exp2(m_prev - m_new)
            p = jnp.exp2(sc - jnp.tile(m_new, (1, sc.shape[1] // LANES)))
            l_scr[s] = alpha * l_scr[s] + lane_fold(p, jnp.add)
            acc_scr[s] = (jnp.tile(alpha, (1, v.shape[1] // LANES)) * acc_scr[s]
                          + jnp.dot(p.astype(bf16), v, preferred_element_type=f32))
            m_scr[s] = m_new

    @pl.when(ki < qi)
    def _():
        step(False)

    @pl.when(ki == qi)
    def _():
        step(True)
        lp = lam_ref[...]
        lam = (jnp.exp(jnp.sum(lp[0:1] * lp[1:2], axis=-1, keepdims=True))
               - jnp.exp(jnp.sum(lp[2:3] * lp[3:4], axis=-1, keepdims=True)) + lambda_init)
        l0 = jnp.sum(l_scr[0], axis=-1, keepdims=True)
        l1 = jnp.sum(l_scr[1], axis=-1, keepdims=True)
        o = acc_scr[0] / l0 - lam * (acc_scr[1] / l1)
        o = o * lax.rsqrt(jnp.mean(o * o, axis=-1, keepdims=True) + DIFF_SUBLN_EPS)
        o_ref[...] = ((o * g_ref[...]) * (1.0 - lambda_init)).astype(o_ref.dtype)


def _diff_attention(proj, lam_p, subln_g, bsz, seq, q_col0, k_col0, v_col0, lambda_init):
    n = proj.shape[0]
    hd = lam_p.shape[1]
    vd = 2 * hd
    tq = min(512, seq)
    nq = seq // tq
    pairs = [(qi, ki) for qi in range(nq) for ki in range(qi + 1)]
    q_of = jnp.asarray([p[0] for p in pairs], jnp.int32)
    k_of = jnp.asarray([p[1] for p in pairs], jnp.int32)
    kern = functools.partial(_attn_kernel, lambda_init=lambda_init, hd=hd)
    return pl.pallas_call(
        kern,
        grid_spec=pltpu.PrefetchScalarGridSpec(
            num_scalar_prefetch=2,
            grid=(bsz, DIFF_HEADS, len(pairs)),
            in_specs=[
                pl.BlockSpec((tq, vd), lambda b, h, t, qt, kt: (b * nq + qt[t], q_col0 // vd + h)),
                pl.BlockSpec((tq, vd), lambda b, h, t, qt, kt: (b * nq + kt[t], k_col0 // vd + h)),
                pl.BlockSpec((tq, vd), lambda b, h, t, qt, kt: (b * nq + kt[t], v_col0 // vd + h)),
                pl.BlockSpec((4, hd), lambda b, h, t, qt, kt: (0, 0)),
                pl.BlockSpec((1, vd), lambda b, h, t, qt, kt: (0, 0)),
            ],
            out_specs=pl.BlockSpec((tq, vd), lambda b, h, t, qt, kt: (b * nq + qt[t], h)),
            scratch_shapes=[pltpu.VMEM((2, tq, LANES), f32), pltpu.VMEM((2, tq, LANES), f32),
                            pltpu.VMEM((2, tq, vd), f32)],
        ),
        out_shape=jax.ShapeDtypeStruct((n, DIFF_HEADS * vd), bf16),
        compiler_params=_cparams(3),
        name="diff_attention",
    )(q_of, k_of, proj, proj, proj, lam_p, subln_g.reshape(1, vd))


def _loga_kernel(h_ref, w1_ref, w2_ref, b_ref, o_ref):
    rowi = lax.broadcasted_iota(jnp.int32, w1_ref.shape, 0)
    w1 = jnp.where(rowi < GLA_RANK, w1_ref[...], 0.0).astype(bf16)
    g1 = lax.dot_general(h_ref[...], w1, (((1,), (1,)), ((), ())), preferred_element_type=f32)
    gpre = jnp.dot(g1.astype(bf16), w2_ref[...].astype(bf16), preferred_element_type=f32) + b_ref[...]
    nx = -gpre
    softplus = jnp.maximum(nx, 0.0) + jnp.log1p(jnp.exp(-jnp.abs(nx)))
    o_ref[...] = -softplus / GLA_TAU


def _log_decay(h, w_in_t, col0, w2, bias):
    n, d = h.shape
    kdim = w2.shape[1]
    bm = min(512, n)
    w2p = jnp.zeros((LANES, kdim), f32).at[:GLA_RANK].set(w2)
    return pl.pallas_call(
        _loga_kernel,
        grid=(n // bm,),
        in_specs=[
            pl.BlockSpec((bm, d), lambda i: (i, 0)),
            pl.BlockSpec((None, LANES, d), lambda i: (0, col0 // LANES, 0)),
            pl.BlockSpec((LANES, kdim), lambda i: (0, 0)),
            pl.BlockSpec((1, kdim), lambda i: (0, 0)),
        ],
        out_specs=pl.BlockSpec((bm, kdim), lambda i: (i, 0)),
        out_shape=jax.ShapeDtypeStruct((n, kdim), f32),
        compiler_params=_cparams(1),
        name="gla_log_decay",
    )(h, w_in_t, w2p, bias.reshape(1, kdim))


def _gla_kernel(q_ref, k_ref, v_ref, la_ref, r_ref, g_ref, o_ref, st_ref, *, scale):
    c = pl.program_id(2)

    @pl.when(c == 0)
    def _():
        st_ref[...] = jnp.zeros(st_ref.shape, f32)

    ch = q_ref.shape[0]
    dv = v_ref.shape[1]
    la = la_ref[...]
    row = lax.broadcasted_iota(jnp.int32, (ch, ch), 0)
    col = lax.broadcasted_iota(jnp.int32, (ch, ch), 1)
    causal = row >= col
    hi = lax.Precision.HIGHEST
    b = jnp.dot(causal.astype(f32), la, preferred_element_type=f32, precision=hi)
    b_last = b[ch - 1:ch, :]
    b_mid = b[ch // 2 - 1:ch // 2, :]
    b_last_col = lax.dot_general(la, jnp.ones((ch, LANES), f32), (((0,), (0,)), ((), ())),
                                 preferred_element_type=f32, precision=hi)
    q = q_ref[...].astype(f32) * scale
    k = k_ref[...].astype(f32)
    q_t = (q * jnp.exp(b)).astype(bf16)
    q_rel = (q * jnp.exp(b - b_mid)).astype(bf16)
    k_rel = (k * jnp.exp(b_mid - b)).astype(bf16)
    k_dec = (k * jnp.exp(b_last - b)).astype(bf16)
    v = v_ref[...]
    attn = lax.dot_general(q_rel, k_rel, (((1,), (1,)), ((), ())), preferred_element_type=f32)
    attn = jnp.where(causal, attn, 0.0)
    st = st_ref[...]
    o = (jnp.dot(attn.astype(bf16), v, preferred_element_type=f32)
         + jnp.dot(q_t, st.astype(bf16), preferred_element_type=f32))
    kv = lax.dot_general(k_dec, v, (((0,), (0,)), ((), ())), preferred_element_type=f32)
    st_ref[...] = jnp.tile(jnp.exp(b_last_col), (1, dv // LANES)) * st + kv
    o = o * lax.rsqrt(jnp.mean(o * o, axis=-1, keepdims=True) + NORM_EPS) * g_ref[...]
    o_ref[...] = (o * _silu(r_ref[...].astype(f32))).astype(o_ref.dtype)


def _gla(proj, log_a, norm_g, bsz, seq, dk, dv):
    n = proj.shape[0]
    ch = 2 * GLA_CHUNK if seq % (2 * GLA_CHUNK) == 0 else GLA_CHUNK
    nc = seq // ch
    kcols = GLA_HEADS * dk
    vcols = GLA_HEADS * dv
    rowb = lambda b, h, c: b * nc + c
    kern = functools.partial(_gla_kernel, scale=dk ** -0.5)
    return pl.pallas_call(
        kern,
        grid=(bsz, GLA_HEADS, nc),
        in_specs=[
            pl.BlockSpec((ch, dk), lambda b, h, c: (rowb(b, h, c), h)),
            pl.BlockSpec((ch, dk), lambda b, h, c: (rowb(b, h, c), kcols // dk + h)),
            pl.BlockSpec((ch, dv), lambda b, h, c: (rowb(b, h, c), 2 * kcols // dv + h)),
            pl.BlockSpec((ch, dk), lambda b, h, c: (rowb(b, h, c), h)),
            pl.BlockSpec((ch, dv), lambda b, h, c: (rowb(b, h, c), (2 * kcols + vcols) // dv + h)),
            pl.BlockSpec((1, dv), lambda b, h, c: (0, 0)),
        ],
        out_specs=pl.BlockSpec((ch, dv), lambda b, h, c: (rowb(b, h, c), h)),
        out_shape=jax.ShapeDtypeStruct((n, vcols), bf16),
        scratch_shapes=[pltpu.VMEM((dk, dv), f32)],
        compiler_params=_cparams(3),
        name="gla_chunked",
    )(proj, proj, proj, log_a, proj, norm_g.reshape(1, dv))


GATHER_UNROLL = 8


def _slab_copy(src_hbm, dst_vmem, sem, src_row, dst_row):
    return pltpu.make_async_copy(src_hbm.at[pl.ds(src_row, 1)], dst_vmem.at[pl.ds(dst_row, 1)], sem)


def _issue_slabs(idx_ref, n_idx, src_hbm, dst, sem):
    def body(r2, carry):
        for u in range(2):
            r = 2 * r2 + u
            _slab_copy(src_hbm, dst, sem, idx_ref[0, r], r).start(priority=u)
        return carry

    lax.fori_loop(0, n_idx // 2, body, 0, unroll=GATHER_UNROLL // 2)


def _wait_slabs(n_idx, src_hbm, dst, sem):
    def body(r, carry):
        _slab_copy(src_hbm, dst, sem, 0, r).wait()
        return carry

    lax.fori_loop(0, n_idx, body, 0, unroll=GATHER_UNROLL)


def _fetch_slabs(cur_ref, nxt_ref, src_hbm, buf, sem, n_idx):
    s = pl.program_id(0)
    slot = s % 2

    @pl.when(s == 0)
    def _():
        _issue_slabs(cur_ref, n_idx, src_hbm, buf.at[0], sem.at[0])

    @pl.when(s + 1 < pl.num_programs(0))
    def _():
        _issue_slabs(nxt_ref, n_idx, src_hbm, buf.at[1 - slot], sem.at[1 - slot])

    _wait_slabs(n_idx, src_hbm, buf.at[slot], sem.at[slot])
    return slot


def _idx_specs(n_steps, n_idx):
    return [
        pl.BlockSpec((None, 1, n_idx), lambda i: (i, 0, 0), memory_space=pltpu.SMEM),
        pl.BlockSpec((None, 1, n_idx), lambda i: (jnp.minimum(i + 1, n_steps - 1), 0, 0),
                     memory_space=pltpu.SMEM),
    ]


def _gather_kernel(cur_ref, nxt_ref, h_hbm, o_ref, buf, sem, *, tr):
    slot = _fetch_slabs(cur_ref, nxt_ref, h_hbm, buf, sem, tr)
    o_ref[...] = buf[slot].reshape(o_ref.shape).astype(o_ref.dtype)


def _gather_rows(h, row_token, tr):
    n, nslab, _ = h.shape
    r = row_token.shape[0]
    kern = functools.partial(_gather_kernel, tr=tr)
    idx = row_token.reshape(r // tr, 1, tr)
    return pl.pallas_call(
        kern,
        grid=(r // tr,),
        in_specs=_idx_specs(r // tr, tr) + [pl.BlockSpec(memory_space=pl.ANY)],
        out_specs=pl.BlockSpec((tr, nslab * LANES), lambda i: (i, 0)),
        out_shape=jax.ShapeDtypeStruct((r, nslab * LANES), bf16),
        scratch_shapes=[pltpu.VMEM((2, tr, nslab, LANES), f32), pltpu.SemaphoreType.DMA((2,))],
        compiler_params=_cparams(1),
        name="moe_gather",
    )(idx, idx, h)


def _combine_kernel(cur_ref, nxt_ref, y_hbm, w_ref, x_ref, gate_ref, fn_ref, o_ref, buf, sem, *, tt):
    slot = _fetch_slabs(cur_ref, nxt_ref, y_hbm, buf, sem, TOP_K * tt)
    ys = buf[slot].reshape(TOP_K * tt, x_ref.shape[1])
    w = w_ref[...]
    moe = w[:, 0:1] * ys[0:tt] + w[:, 1:2] * ys[tt:2 * tt]
    x = x_ref[...] + gate_ref[...] * moe
    y = x * lax.rsqrt(jnp.mean(x * x, axis=-1, keepdims=True) + NORM_EPS)
    o_ref[...] = y * fn_ref[...]


def _combine(y, pos, top_w, x, mod, i_gate, final_norm, seq):
    n, d = x.shape
    nslab = d // LANES
    tt = min(128, seq)
    kern = functools.partial(_combine_kernel, tt=tt)
    idx = pos.reshape(n // tt, tt, TOP_K).transpose(0, 2, 1).reshape(n // tt, 1, TOP_K * tt)
    return pl.pallas_call(
        kern,
        grid=(n // tt,),
        in_specs=_idx_specs(n // tt, TOP_K * tt) + [
            pl.BlockSpec(memory_space=pl.ANY),
            pl.BlockSpec((tt, LANES), lambda i: (i, 0)),
            pl.BlockSpec((tt, d), lambda i: (i, 0)),
            pl.BlockSpec((None, None, 1, d), lambda i: (i * tt // seq, i_gate, 0, 0)),
            pl.BlockSpec((1, d), lambda i: (0, 0)),
        ],
        out_specs=pl.BlockSpec((tt, d), lambda i: (i, 0)),
        out_shape=jax.ShapeDtypeStruct((n, d), f32),
        scratch_shapes=[pltpu.VMEM((2, TOP_K * tt, nslab, LANES), f32),
                        pltpu.SemaphoreType.DMA((2,))],
        compiler_params=_cparams(1),
        name="moe_combine_final_norm",
    )(idx, idx, y, top_w, x, mod, final_norm.reshape(1, d))


def _routing_tables(top_i, bm):
    n = top_i.shape[0]
    e_flat = top_i.reshape(-1)
    onehot = (e_flat[:, None] == jnp.arange(N_EXPERTS)[None, :]).astype(jnp.int32)
    counts = jnp.sum(onehot, axis=0)
    rank = jnp.take_along_axis(jnp.cumsum(onehot, axis=0) - onehot, e_flat[:, None], axis=1)[:, 0]
    padded = ((counts + bm - 1) // bm) * bm
    ends = jnp.cumsum(padded)
    starts = ends - padded
    pos = starts[e_flat] + rank
    n_rows = n * TOP_K + N_EXPERTS * bm
    n_tiles = n_rows // bm
    num_tiles = (ends[-1] // bm).astype(jnp.int32)
    tile_start = jnp.minimum(jnp.arange(n_tiles), num_tiles - 1) * bm
    tile_group = jnp.minimum(jnp.searchsorted(ends, tile_start, side="right"),
                             N_EXPERTS - 1).astype(jnp.int32)
    row_token = jnp.zeros((n_rows,), jnp.int32).at[pos].set(jnp.arange(n * TOP_K, dtype=jnp.int32) // TOP_K)
    return pos.astype(jnp.int32), row_token, tile_group, num_tiles.reshape(1)


def _rope_tables(seq, hd):
    inv = ROPE_THETA ** (-jnp.arange(0, hd, 2, dtype=f32) / hd)
    ang = jnp.arange(seq).astype(f32)[:, None] * inv[None, :]
    cos, sin = jnp.cos(ang), jnp.sin(ang)
    return jnp.concatenate([cos, cos], axis=-1), jnp.concatenate([-sin, sin], axis=-1)


def kernel(x, c, norm_gains, ada_w, ada_b, e_w_in, e_conv_w, e_conv_b, e_conv_ln_g, e_conv_ln_b, e_diff_lambda, e_diff_subln, e_w_out, e_ffn_gate, e_ffn_up, e_ffn_down, o_w_in, o_gate_w2, o_gate_b, o_gla_norm, o_w_out, o_router, o_exp_gate, o_exp_up, o_exp_down, final_norm):
    bsz, seq, d = x.shape
    n = bsz * seq
    depth = ada_w.shape[0]
    assert depth == 2, "trunk is one even (conv + diff-attn) and one odd (GLA + experts) layer"
    xf = x.reshape(n, d)
    mods = _ada(c, ada_w, ada_b)
    bm = min(1024, seq)

    mod = mods[0]
    conv_ch = e_conv_w.shape[2]
    hd = e_diff_lambda.shape[2]
    qk_cols = 2 * DIFF_HEADS * hd
    q_col0 = 2 * conv_ch
    k_col0 = q_col0 + qk_cols
    v_col0 = k_col0 + qk_cols
    even_in = e_w_in.shape[2]
    bn = 512
    h = _normmod(xf, norm_gains[0, 0], mod, 0, 1, seq)
    cos_t, sin_t = _rope_tables(seq, hd)
    bmi = min(512, seq)
    bni = max(t for t in (1024, 512, 256)
              if all(cc % t == 0 for cc in (q_col0, k_col0, v_col0, even_in)))
    rope_spec = pl.BlockSpec(
        (bmi, hd), lambda j, i, te, nt: (jnp.minimum(i, nt[0] - 1) % (seq // bmi), 0))
    ep = functools.partial(_ep_rope, q_tiles=(q_col0 // bni, k_col0 // bni),
                           k_tiles=(k_col0 // bni, v_col0 // bni),
                           q_scale=hd ** -0.5 * math.log2(math.e))
    proj = _gmm([h], [e_w_in], bm=bmi, bn=bni, n_cols=even_in, epilogue=ep,
                extras=(cos_t, sin_t), extra_specs=(rope_spec, rope_spec),
                out_dtype=bf16, name="even_in_proj")
    y_a = _conv(proj, e_conv_w[0], e_conv_b[0], e_conv_ln_g[0], e_conv_ln_b[0], seq)
    lambda_init = 0.8 - 0.6 * math.exp(-0.3 * 0)
    y_b = _diff_attention(proj, e_diff_lambda[0], e_diff_subln[0], bsz, seq,
                          q_col0, k_col0, v_col0, lambda_init)
    xf = _gmm([y_a, y_b], [e_w_out], bm=bm, bn=bn, n_cols=d, epilogue=_ep_resid,
              extras=(xf, mod), extra_specs=(_tile_spec(bm, bn), _gate_spec(bm, bn, seq, 2)),
              out_dtype=f32, name="even_out_proj")
    h = _normmod(xf, norm_gains[0, 1], mod, 3, 4, seq)
    d_ff = e_ffn_gate.shape[2]
    bnf = 256
    act = _gmm([h], [e_ffn_gate, e_ffn_up], bm=bm, bn=bnf, n_cols=d_ff, epilogue=_ep_swiglu,
               out_dtype=bf16, name="ffn_gate_up")
    half = d_ff // 2
    bmd = min(512, seq)
    part = _gmm([act], [e_ffn_down], bm=bmd, bn=bn, n_cols=d, k_block=half, k_index=0,
                epilogue=_ep_plain, out_dtype=f32, name="ffn_down_lo")
    xf = _gmm([act], [e_ffn_down], bm=bmd, bn=bn, n_cols=d, k_block=half, k_index=1,
              epilogue=_ep_resid_partial,
              extras=(xf, mod, part),
              extra_specs=(_tile_spec(bmd, bn), _gate_spec(bmd, bn, seq, 5), _tile_spec(bmd, bn)),
              out_dtype=f32, name="ffn_down_hi")

    mod = mods[1]
    dk = o_gate_w2.shape[2] // GLA_HEADS
    dv = o_gla_norm.shape[1]
    kcols = GLA_HEADS * dk
    vcols = GLA_HEADS * dv
    main_cols = 2 * kcols + 2 * vcols
    h = _normmod(xf, norm_gains[1, 0], mod, 0, 1, seq)
    w_in_t = jnp.swapaxes(o_w_in, 1, 2)
    proj = _gmm([h], [w_in_t], bm=bmi, bn=1024, n_cols=main_cols, epilogue=_ep_plain,
                out_dtype=bf16, w_t=True, name="odd_in_proj")
    log_a = _log_decay(h, w_in_t, main_cols, o_gate_w2[0], o_gate_b[0])
    o = _gla(proj, log_a, o_gla_norm[0], bsz, seq, dk, dv)
    xf = _gmm([o], [o_w_out], bm=bmi, bn=bn, n_cols=d, epilogue=_ep_resid,
              extras=(xf, mod), extra_specs=(_tile_spec(bmi, bn), _gate_spec(bmi, bn, seq, 2)),
              out_dtype=f32, name="odd_out_proj")
    h32, top_i, top_w = _normmod_router(xf, norm_gains[1, 1], mod, 3, 4, seq, o_router[0])
    bme = min(512, seq)
    pos, row_token, tile_group, num_tiles = _routing_tables(top_i, bme)
    hs = _gather_rows(h32, row_token, min(256, bme))
    d_fe = o_exp_gate.shape[3]
    act = _gmm([hs], [o_exp_gate[0], o_exp_up[0]], bm=bme, bn=512, n_cols=d_fe,
               epilogue=_ep_swiglu, out_dtype=bf16, tile_group=tile_group, num_tiles=num_tiles,
               name="expert_gate_up")
    ye = _gmm([act], [o_exp_down[0]], bm=bme, bn=1024, n_cols=d, epilogue=_ep_slabs,
              out_dtype=f32, tile_group=tile_group, num_tiles=num_tiles, slab_out=True,
              name="expert_down")
    out = _combine(ye, pos, top_w, xf, mod, 5, final_norm, seq)
    return out.reshape(bsz, seq, d)
```

```python
import functools
import math

import jax
import jax.numpy as jnp
from jax import lax
from jax.experimental import pallas as pl
from jax.experimental.pallas import tpu as pltpu

f32 = jnp.float32
bf16 = jnp.bfloat16

N_MOD = 6
NORM_EPS = 1e-6
CONV_WIDTH = 31
CONV_HALO = 32
DIFF_HEADS = 8
DIFF_SUBLN_EPS = 1e-5
ROPE_THETA = 10000.0
GLA_HEADS = 4
GLA_RANK = 16
GLA_TAU = 16.0
GLA_CHUNK = 64
N_EXPERTS = 8
TOP_K = 2
LANES = 128
VMEM_LIMIT_BYTES = 58 * 1024 * 1024


def _cparams(n_axes):
    return pltpu.CompilerParams(
        dimension_semantics=("arbitrary",) * n_axes,
        vmem_limit_bytes=VMEM_LIMIT_BYTES)


def _sigmoid(x):
    return 1.0 / (1.0 + jnp.exp(-x))


def _silu(x):
    return x * _sigmoid(x)


def _ada_kernel(c_ref, w_ref, b_ref, o_ref):
    ca = _silu(c_ref[...]).astype(bf16)
    w = w_ref[...].astype(bf16)
    o_ref[...] = jnp.dot(ca, w, preferred_element_type=f32) + b_ref[...]


def _ada(c, ada_w, ada_b):
    depth, d, n6 = ada_w.shape
    bsz = c.shape[0]
    rows = 8
    cp = jnp.zeros((rows, d), f32).at[:bsz].set(c)
    bn = min(512, n6)
    out = pl.pallas_call(
        _ada_kernel,
        grid=(depth, n6 // bn),
        in_specs=[
            pl.BlockSpec((rows, d), lambda l, j: (0, 0)),
            pl.BlockSpec((None, d, bn), lambda l, j: (l, 0, j)),
            pl.BlockSpec((None, 1, bn), lambda l, j: (l, 0, j)),
        ],
        out_specs=pl.BlockSpec((None, rows, bn), lambda l, j: (l, 0, j)),
        out_shape=jax.ShapeDtypeStruct((depth, rows, n6), f32),
        compiler_params=_cparams(2),
        name="ada_mod",
    )(cp, ada_w, ada_b.reshape(depth, 1, n6))
    return out[:, :bsz].reshape(depth, bsz, N_MOD, 1, d)


def _normmod_value(x_ref, g_ref, sh_ref, sc_ref):
    x = x_ref[...]
    y = x * lax.rsqrt(jnp.mean(x * x, axis=-1, keepdims=True) + NORM_EPS)
    y = y * g_ref[...]
    return y * (1.0 + sc_ref[...]) + sh_ref[...]


def _normmod_kernel(x_ref, g_ref, sh_ref, sc_ref, o_ref):
    o_ref[...] = _normmod_value(x_ref, g_ref, sh_ref, sc_ref).astype(o_ref.dtype)


def _normmod_specs(bm, d, seq, i_shift, i_scale):
    return [
        pl.BlockSpec((bm, d), lambda i: (i, 0)),
        pl.BlockSpec((1, d), lambda i: (0, 0)),
        pl.BlockSpec((None, None, 1, d), lambda i: (i * bm // seq, i_shift, 0, 0)),
        pl.BlockSpec((None, None, 1, d), lambda i: (i * bm // seq, i_scale, 0, 0)),
    ]


def _normmod(x, gain, mod, i_shift, i_scale, seq):
    n, d = x.shape
    bm = min(512, seq)
    return pl.pallas_call(
        _normmod_kernel,
        grid=(n // bm,),
        in_specs=_normmod_specs(bm, d, seq, i_shift, i_scale),
        out_specs=pl.BlockSpec((bm, d), lambda i: (i, 0)),
        out_shape=jax.ShapeDtypeStruct((n, d), bf16),
        compiler_params=_cparams(1),
        name="normmod",
    )(x, gain.reshape(1, d), mod, mod)


def _store_slabs(o_ref, val):
    o_ref[...] = val.reshape(o_ref.shape)


HI16 = 0xFFFF0000


def _pack_bf16_pairs(v):
    half = v.shape[1] // 2
    lo = pltpu.bitcast(v[:, :half].astype(bf16).astype(f32), jnp.uint32) >> 16
    hi = pltpu.bitcast(v[:, half:].astype(bf16).astype(f32), jnp.uint32) & jnp.uint32(HI16)
    return lo | hi


def _unpack_bf16_pairs(w):
    lo = pltpu.bitcast(w << 16, f32).astype(bf16)
    hi = pltpu.bitcast(w & jnp.uint32(HI16), f32).astype(bf16)
    return jnp.concatenate([lo, hi], axis=1)


def _normmod_router_kernel(x_ref, g_ref, sh_ref, sc_ref, wr_ref, h_ref, idx_ref, wgt_ref):
    h = _normmod_value(x_ref, g_ref, sh_ref, sc_ref)
    _store_slabs(h_ref, _pack_bf16_pairs(h))
    wr = wr_ref[...]
    h_hi = h.astype(bf16)
    h_lo = (h - h_hi.astype(f32)).astype(bf16)
    w_hi = wr.astype(bf16)
    w_lo = (wr - w_hi.astype(f32)).astype(bf16)
    logits = (jnp.dot(h_hi, w_hi, preferred_element_type=f32)
              + jnp.dot(h_hi, w_lo, preferred_element_type=f32)
              + jnp.dot(h_lo, w_hi, preferred_element_type=f32))
    col = lax.broadcasted_iota(jnp.int32, logits.shape, 1)
    neg = jnp.float32(-jnp.inf)
    l0 = jnp.where(col < N_EXPERTS, logits, neg)
    m1 = jnp.max(l0, axis=-1, keepdims=True)
    i1 = jnp.min(jnp.where(l0 == m1, col, LANES), axis=-1, keepdims=True)
    l1 = jnp.where(col == i1, neg, l0)
    m2 = jnp.max(l1, axis=-1, keepdims=True)
    i2 = jnp.min(jnp.where(l1 == m2, col, LANES), axis=-1, keepdims=True)
    e = jnp.exp(m2 - m1)
    w1 = 1.0 / (1.0 + e)
    w2 = e / (1.0 + e)
    idx_ref[...] = jnp.where(col == 0, i1, jnp.where(col == 1, i2, 0))
    wgt_ref[...] = jnp.where(col == 0, w1, jnp.where(col == 1, w2, 0.0))


def _normmod_router(x, gain, mod, i_shift, i_scale, seq, w_router):
    n, d = x.shape
    bm = min(256, seq)
    wr = jnp.zeros((d, LANES), f32).at[:, :N_EXPERTS].set(w_router)
    h, idx, wgt = pl.pallas_call(
        _normmod_router_kernel,
        grid=(n // bm,),
        in_specs=_normmod_specs(bm, d, seq, i_shift, i_scale)
        + [pl.BlockSpec((d, LANES), lambda i: (0, 0))],
        out_specs=[
            pl.BlockSpec((bm, d // 2 // LANES, LANES), lambda i: (i, 0, 0)),
            pl.BlockSpec((bm, LANES), lambda i: (i, 0)),
            pl.BlockSpec((bm, LANES), lambda i: (i, 0)),
        ],
        out_shape=[
            jax.ShapeDtypeStruct((n, d // 2 // LANES, LANES), jnp.uint32),
            jax.ShapeDtypeStruct((n, LANES), jnp.int32),
            jax.ShapeDtypeStruct((n, LANES), f32),
        ],
        compiler_params=_cparams(1),
        name="normmod_router",
    )(x, gain.reshape(1, d), mod, mod, wr)
    return h, idx[:, :TOP_K], wgt


def _gmm_kernel(te_ref, nt_ref, *refs, n_x, n_w, n_extra, k_parts, epilogue, w_t, n_tiles, bn, k_row0):
    x_refs = refs[:n_x]
    w_hbm = refs[n_x:n_x + n_w]
    e_refs = refs[n_x + n_w:n_x + n_w + n_extra]
    o_ref = refs[n_x + n_w + n_extra]
    wf_ref, wb_ref, sem = refs[n_x + n_w + n_extra + 1:]
    j = pl.program_id(0)
    i = pl.program_id(1)
    active = i < nt_ref[0]
    prev = te_ref[jnp.maximum(i - 1, 0)]
    changed = jnp.logical_and(active, jnp.logical_or(i == 0, te_ref[i] != prev))
    k = sum(k_parts)

    def w_copy(t, g, jj):
        c0 = pl.multiple_of(jj * bn, bn)
        if w_t:
            src = w_hbm[t].at[g, pl.ds(c0, bn), pl.ds(k_row0, k)]
        else:
            src = w_hbm[t].at[g, pl.ds(k_row0, k), pl.ds(c0, bn)]
        return pltpu.make_async_copy(src, wf_ref.at[t], sem.at[0])

    @pl.when(changed)
    def _():
        run = j * nt_ref[1] + te_ref[2 * n_tiles + i]
        g = te_ref[i]

        @pl.when(run == 0)
        def _():
            for t in range(n_w):
                w_copy(t, g, j).start()

        for t in range(n_w):
            w_copy(t, g, j).wait()
        for t in range(n_w):
            wb_ref[t] = wf_ref[t].astype(bf16)
        nxt = te_ref[n_tiles + i]
        more_groups = nxt >= 0
        g2 = jnp.where(more_groups, te_ref[jnp.maximum(nxt, 0)], te_ref[0])
        j2 = jnp.where(more_groups, j, j + 1)

        @pl.when(j2 < pl.num_programs(0))
        def _():
            for t in range(n_w):
                w_copy(t, g2, j2).start()

    @pl.when(active)
    def _():
        accs = []
        for t in range(n_w):
            acc = None
            off = 0
            for p in range(n_x):
                kp = k_parts[p]
                if w_t:
                    part = lax.dot_general(x_refs[p][...], wb_ref[t, :, off:off + kp],
                                           (((1,), (1,)), ((), ())), preferred_element_type=f32)
                else:
                    part = jnp.dot(x_refs[p][...], wb_ref[t, off:off + kp, :],
                                   preferred_element_type=f32)
                acc = part if acc is None else acc + part
                off += kp
            accs.append(acc)
        epilogue(accs, e_refs, o_ref)

    @pl.when(jnp.logical_not(active))
    def _():
        o_ref[...] = jnp.zeros(o_ref.shape, o_ref.dtype)


def _gmm(xs, ws, *, bm, bn, n_cols, k_block=None, k_index=0, epilogue, extras=(), extra_specs=(),
         out_dtype, tile_group=None, num_tiles=None, slab_out=False, w_t=False, name):
    m = xs[0].shape[0]
    k_parts = tuple(x.shape[1] for x in xs) if k_block is None else (k_block,)
    k = sum(k_parts)
    n_tiles = m // bm
    if tile_group is None:
        tile_group = jnp.zeros((n_tiles,), jnp.int32)
        num_tiles = jnp.full((1,), n_tiles, jnp.int32)
    xk = k_index if k_block is not None else 0
    tid = jnp.arange(n_tiles, dtype=jnp.int32)
    first = jnp.logical_and(tid < num_tiles[0],
                            jnp.concatenate([jnp.ones((1,), bool), tile_group[1:] != tile_group[:-1]]))
    ordinal = jnp.cumsum(first.astype(jnp.int32)) - 1
    first_at = jnp.where(first, tid, n_tiles)
    nxt = jnp.concatenate([lax.cummin(first_at[::-1])[::-1][1:], jnp.full((1,), n_tiles, jnp.int32)])
    nxt = jnp.where(nxt >= n_tiles, -1, nxt).astype(jnp.int32)
    tables = jnp.concatenate([tile_group.astype(jnp.int32), nxt, ordinal.astype(jnp.int32)])
    counts = jnp.stack([num_tiles[0], ordinal[jnp.maximum(num_tiles[0] - 1, 0)] + 1]).astype(jnp.int32)

    def row(i, nt):
        return jnp.minimum(i, nt[0] - 1)

    in_specs = [pl.BlockSpec((bm, kp), lambda j, i, te, nt: (row(i, nt), xk)) for kp in k_parts]
    in_specs += [pl.BlockSpec(memory_space=pl.ANY) for _ in ws]
    in_specs += list(extra_specs)
    w_tile = (bn, k) if w_t else (k, bn)
    kern = functools.partial(_gmm_kernel, n_x=len(xs), n_w=len(ws), n_extra=len(extras),
                             k_parts=k_parts, epilogue=epilogue, w_t=w_t, n_tiles=n_tiles, bn=bn,
                             k_row0=xk * k)
    if slab_out:
        out_spec = pl.BlockSpec((bm, bn // LANES, LANES), lambda j, i, te, nt: (i, j, 0))
        out_shape = jax.ShapeDtypeStruct((m, n_cols // LANES, LANES), out_dtype)
    else:
        out_spec = pl.BlockSpec((bm, bn), lambda j, i, te, nt: (i, j))
        out_shape = jax.ShapeDtypeStruct((m, n_cols), out_dtype)
    return pl.pallas_call(
        kern,
        grid_spec=pltpu.PrefetchScalarGridSpec(
            num_scalar_prefetch=2,
            grid=(n_cols // bn, n_tiles),
            in_specs=in_specs,
            out_specs=out_spec,
            scratch_shapes=[pltpu.VMEM((len(ws),) + w_tile, f32),
                            pltpu.VMEM((len(ws),) + w_tile, bf16),
                            pltpu.SemaphoreType.DMA((1,))],
        ),
        out_shape=out_shape,
        compiler_params=_cparams(2),
        name=name,
    )(tables, counts, *xs, *ws, *extras)


def _ep_plain(accs, e_refs, o_ref):
    o_ref[...] = accs[0].astype(o_ref.dtype)


def _ep_swiglu(accs, e_refs, o_ref):
    g, u = accs
    o_ref[...] = (_silu(g) * u).astype(o_ref.dtype)


def _ep_resid(accs, e_refs, o_ref):
    res_ref, gate_ref = e_refs
    o_ref[...] = res_ref[...] + gate_ref[...] * accs[0]


def _ep_resid_partial(accs, e_refs, o_ref):
    res_ref, gate_ref, part_ref = e_refs
    o_ref[...] = res_ref[...] + gate_ref[...] * (part_ref[...] + accs[0])


def _ep_slabs(accs, e_refs, o_ref):
    _store_slabs(o_ref, accs[0])


def _ep_rope(accs, e_refs, o_ref, *, q_tiles, k_tiles, q_scale):
    cos_ref, sin_ref = e_refs
    acc = accs[0]
    j = pl.program_id(0)
    is_q = jnp.logical_and(j >= q_tiles[0], j < q_tiles[1])
    is_k = jnp.logical_and(j >= k_tiles[0], j < k_tiles[1])

    def roped(scale):
        cos = cos_ref[...]
        sin = sin_ref[...]
        outs = []
        for hh in range(acc.shape[1] // LANES):
            a = acc[:, hh * LANES:(hh + 1) * LANES]
            rot = pltpu.roll(a, LANES // 2, 1)
            outs.append((a * cos + rot * sin) * scale)
        return jnp.concatenate(outs, axis=1)

    @pl.when(is_q)
    def _():
        o_ref[...] = roped(q_scale).astype(o_ref.dtype)

    @pl.when(is_k)
    def _():
        o_ref[...] = roped(1.0).astype(o_ref.dtype)

    @pl.when(jnp.logical_not(jnp.logical_or(is_q, is_k)))
    def _():
        o_ref[...] = acc.astype(o_ref.dtype)


def _gate_spec(bm, bn, seq, i_gate):
    return pl.BlockSpec((None, None, 1, bn),
                        lambda j, i, te, nt: (jnp.minimum(i, nt[0] - 1) * bm // seq, i_gate, 0, j))


def _tile_spec(bm, bn):
    return pl.BlockSpec((bm, bn), lambda j, i, te, nt: (jnp.minimum(i, nt[0] - 1), j))


def _conv_kernel(val_ref, gate_ref, hval_ref, hgate_ref, w_ref, b_ref, lg_ref, lb_ref,
                 o_ref, u_scr, v_scr, *, ts, seq, rows):
    i = pl.program_id(0)
    first = (i * ts) % seq == 0
    hu = hval_ref[...].astype(f32) * _sigmoid(hgate_ref[...].astype(f32))
    u_scr[0:CONV_HALO, :] = jnp.where(first, 0.0, hu)
    u_scr[CONV_HALO:, :] = val_ref[...].astype(f32) * _sigmoid(gate_ref[...].astype(f32))
    lead = CONV_HALO - (CONV_WIDTH - 1)

    cw = LANES
    sub = 8
    wrows = rows + CONV_HALO

    def chunk(c, carry):
        r0 = pl.multiple_of(c * rows, rows)
        for c0 in range(0, u_scr.shape[1], cw):
            win = u_scr[pl.ds(r0, wrows), c0:c0 + cw]
            acc = jnp.zeros((rows, cw), f32)
            for s in range(sub):
                taps = [j for j in range(CONV_WIDTH) if (lead + j) % sub == s]
                if not taps:
                    continue
                shifted = win if s == 0 else pltpu.roll(win, wrows - s, 0)
                for j in taps:
                    a = (lead + j - s)
                    acc = acc + w_ref[j:j + 1, c0:c0 + cw] * shifted[a:a + rows, :]
            v_scr[pl.ds(r0, rows), c0:c0 + cw] = acc
        return carry

    lax.fori_loop(0, ts // rows, chunk, 0)
    u = v_scr[...] + b_ref[...]
    mu = jnp.mean(u, axis=-1, keepdims=True)
    var = jnp.mean(jnp.square(u - mu), axis=-1, keepdims=True)
    y = (u - mu) * lax.rsqrt(var + NORM_EPS) * lg_ref[...] + lb_ref[...]
    o_ref[...] = _silu(y).astype(o_ref.dtype)


def _conv(proj, conv_w, conv_b, ln_g, ln_b, seq):
    n = proj.shape[0]
    ch = conv_w.shape[1]
    ts = min(256, seq)
    hb = ts // CONV_HALO
    halo_row = lambda i: jnp.maximum(i * hb - 1, 0)
    vec = lambda: pl.BlockSpec((1, ch), lambda i: (0, 0))
    kern = functools.partial(_conv_kernel, ts=ts, seq=seq, rows=64)
    return pl.pallas_call(
        kern,
        grid=(n // ts,),
        in_specs=[
            pl.BlockSpec((ts, ch), lambda i: (i, 0)),
            pl.BlockSpec((ts, ch), lambda i: (i, 1)),
            pl.BlockSpec((CONV_HALO, ch), lambda i: (halo_row(i), 0)),
            pl.BlockSpec((CONV_HALO, ch), lambda i: (halo_row(i), 1)),
            pl.BlockSpec((CONV_WIDTH, ch), lambda i: (0, 0)),
            vec(), vec(), vec(),
        ],
        out_specs=pl.BlockSpec((ts, ch), lambda i: (i, 0)),
        out_shape=jax.ShapeDtypeStruct((n, ch), bf16),
        scratch_shapes=[pltpu.VMEM((ts + CONV_HALO, ch), f32), pltpu.VMEM((ts, ch), f32)],
        compiler_params=_cparams(1),
        name="conformer_conv",
    )(proj, proj, proj, proj, conv_w, conv_b.reshape(1, ch), ln_g.reshape(1, ch), ln_b.reshape(1, ch))


def _attn_kernel(qt_ref, kt_ref, q_ref, k_ref, v_ref, lam_ref, g_ref, o_ref, m_scr, l_scr, acc_scr,
                 *, lambda_init, hd):
    t = pl.program_id(2)
    qi = qt_ref[t]
    ki = kt_ref[t]
    tq = q_ref.shape[0]
    tk = k_ref.shape[0]
    last = ki == (qi + 1) * (tq // tk) - 1
    crosses = (ki + 1) * tk - 1 > qi * tq

    @pl.when(ki == 0)
    def _():
        m_scr[...] = jnp.full(m_scr.shape, -jnp.inf, f32)
        l_scr[...] = jnp.zeros(l_scr.shape, f32)
        acc_scr[...] = jnp.zeros(acc_scr.shape, f32)

    def lane_fold(x, op):
        out = x[:, 0:LANES]
        for cb in range(1, x.shape[1] // LANES):
            out = op(out, x[:, cb * LANES:(cb + 1) * LANES])
        return out

    def step(masked):
        v = v_ref[...]
        for s in range(2):
            q = q_ref[:, s * hd:(s + 1) * hd]
            k = k_ref[:, s * hd:(s + 1) * hd]
            sc = lax.dot_general(q, k, (((1,), (1,)), ((), ())), preferred_element_type=f32)
            if masked:
                row = lax.broadcasted_iota(jnp.int32, sc.shape, 0)
                col = lax.broadcasted_iota(jnp.int32, sc.shape, 1)
                sc = jnp.where(col + (ki * tk - qi * tq) <= row, sc, -jnp.inf)
            m_cur = jnp.max(lane_fold(sc, jnp.maximum), axis=-1, keepdims=True)
            m_prev = m_scr[s]
            m_new = jnp.maximum(m_prev, m_cur)
            alpha = jnp.exp2(m_prev - m_new)
            p = jnp.exp2(sc - jnp.tile(m_new, (1, sc.shape[1] // LANES)))
            l_scr[s] = alpha * l_scr[s] + lane_fold(p, jnp.add)
            acc_scr[s] = (jnp.tile(alpha, (1, v.shape[1] // LANES)) * acc_scr[s]
                          + jnp.dot(p.astype(bf16), v, preferred_element_type=f32))
            m_scr[s] = m_new

    @pl.when(jnp.logical_not(crosses))
    def _():
        step(False)

    @pl.when(crosses)
    def _():
        step(True)

    @pl.when(last)
    def _():
        lp = lam_ref[...]
        lam = (jnp.exp(jnp.sum(lp[0:1] * lp[1:2], axis=-1, keepdims=True))
               - jnp.exp(jnp.sum(lp[2:3] * lp[3:4], axis=-1, keepdims=True)) + lambda_init)
        l0 = jnp.sum(l_scr[0], axis=-1, keepdims=True)
        l1 = jnp.sum(l_scr[1], axis=-1, keepdims=True)
        o = acc_scr[0] / l0 - lam * (acc_scr[1] / l1)
        o = o * lax.rsqrt(jnp.mean(o * o, axis=-1, keepdims=True) + DIFF_SUBLN_EPS)
        o_ref[...] = ((o * g_ref[...]) * (1.0 - lambda_init)).astype(o_ref.dtype)


def _diff_attention(proj, lam_p, subln_g, bsz, seq, q_col0, k_col0, v_col0, lambda_init):
    n = proj.shape[0]
    hd = lam_p.shape[1]
    vd = 2 * hd
    tq = min(1024, seq)
    tk = min(512, seq)
    nq = seq // tq
    nk = seq // tk
    pairs = [(qi, ki) for qi in range(nq) for ki in range((qi + 1) * (tq // tk))]
    q_of = jnp.asarray([p[0] for p in pairs], jnp.int32)
    k_of = jnp.asarray([p[1] for p in pairs], jnp.int32)
    kern = functools.partial(_attn_kernel, lambda_init=lambda_init, hd=hd)
    return pl.pallas_call(
        kern,
        grid_spec=pltpu.PrefetchScalarGridSpec(
            num_scalar_prefetch=2,
            grid=(bsz, DIFF_HEADS, len(pairs)),
            in_specs=[
                pl.BlockSpec((tq, vd), lambda b, h, t, qt, kt: (b * nq + qt[t], q_col0 // vd + h)),
                pl.BlockSpec((tk, vd), lambda b, h, t, qt, kt: (b * nk + kt[t], k_col0 // vd + h)),
                pl.BlockSpec((tk, vd), lambda b, h, t, qt, kt: (b * nk + kt[t], v_col0 // vd + h)),
                pl.BlockSpec((4, hd), lambda b, h, t, qt, kt: (0, 0)),
                pl.BlockSpec((1, vd), lambda b, h, t, qt, kt: (0, 0)),
            ],
            out_specs=pl.BlockSpec((tq, vd), lambda b, h, t, qt, kt: (b * nq + qt[t], h)),
            scratch_shapes=[pltpu.VMEM((2, tq, LANES), f32), pltpu.VMEM((2, tq, LANES), f32),
                            pltpu.VMEM((2, tq, vd), f32)],
        ),
        out_shape=jax.ShapeDtypeStruct((n, DIFF_HEADS * vd), bf16),
        compiler_params=_cparams(3),
        name="diff_attention",
    )(q_of, k_of, proj, proj, proj, lam_p, subln_g.reshape(1, vd))


def _loga_kernel(h_ref, w1_ref, w2_ref, b_ref, o_ref):
    rowi = lax.broadcasted_iota(jnp.int32, w1_ref.shape, 0)
    w1 = jnp.where(rowi < GLA_RANK, w1_ref[...], 0.0).astype(bf16)
    g1 = lax.dot_general(h_ref[...], w1, (((1,), (1,)), ((), ())), preferred_element_type=f32)
    gpre = jnp.dot(g1.astype(bf16), w2_ref[...].astype(bf16), preferred_element_type=f32) + b_ref[...]
    nx = -gpre
    softplus = jnp.maximum(nx, 0.0) + jnp.log1p(jnp.exp(-jnp.abs(nx)))
    o_ref[...] = -softplus / GLA_TAU


def _log_decay(h, w_in_t, col0, w2, bias):
    n, d = h.shape
    kdim = w2.shape[1]
    bm = min(512, n)
    w2p = jnp.zeros((LANES, kdim), f32).at[:GLA_RANK].set(w2)
    return pl.pallas_call(
        _loga_kernel,
        grid=(n // bm,),
        in_specs=[
            pl.BlockSpec((bm, d), lambda i: (i, 0)),
            pl.BlockSpec((None, LANES, d), lambda i: (0, col0 // LANES, 0)),
            pl.BlockSpec((LANES, kdim), lambda i: (0, 0)),
            pl.BlockSpec((1, kdim), lambda i: (0, 0)),
        ],
        out_specs=pl.BlockSpec((bm, kdim), lambda i: (i, 0)),
        out_shape=jax.ShapeDtypeStruct((n, kdim), f32),
        compiler_params=_cparams(1),
        name="gla_log_decay",
    )(h, w_in_t, w2p, bias.reshape(1, kdim))


def _gla_kernel(q_ref, k_ref, v_ref, la_ref, r_ref, g_ref, o_ref, st_ref, *, scale):
    c = pl.program_id(2)

    @pl.when(c == 0)
    def _():
        st_ref[...] = jnp.zeros(st_ref.shape, f32)

    ch = q_ref.shape[0]
    dv = v_ref.shape[1]
    la = la_ref[...]
    row = lax.broadcasted_iota(jnp.int32, (ch, ch), 0)
    col = lax.broadcasted_iota(jnp.int32, (ch, ch), 1)
    causal = row >= col
    hi = lax.Precision.HIGHEST
    b = jnp.dot(causal.astype(f32), la, preferred_element_type=f32, precision=hi)
    b_last = b[ch - 1:ch, :]
    b_mid = b[ch // 2 - 1:ch // 2, :]
    b_last_col = lax.dot_general(la, jnp.ones((ch, LANES), f32), (((0,), (0,)), ((), ())),
                                 preferred_element_type=f32, precision=hi)
    q = q_ref[...].astype(f32) * scale
    k = k_ref[...].astype(f32)
    q_t = (q * jnp.exp(b)).astype(bf16)
    q_rel = (q * jnp.exp(b - b_mid)).astype(bf16)
    k_rel = (k * jnp.exp(b_mid - b)).astype(bf16)
    k_dec = (k * jnp.exp(b_last - b)).astype(bf16)
    v = v_ref[...]
    attn = lax.dot_general(q_rel, k_rel, (((1,), (1,)), ((), ())), preferred_element_type=f32)
    attn = jnp.where(causal, attn, 0.0)
    st = st_ref[...]
    o = (jnp.dot(attn.astype(bf16), v, preferred_element_type=f32)
         + jnp.dot(q_t, st.astype(bf16), preferred_element_type=f32))
    kv = lax.dot_general(k_dec, v, (((0,), (0,)), ((), ())), preferred_element_type=f32)
    st_ref[...] = jnp.tile(jnp.exp(b_last_col), (1, dv // LANES)) * st + kv
    o = o * lax.rsqrt(jnp.mean(o * o, axis=-1, keepdims=True) + NORM_EPS) * g_ref[...]
    o_ref[...] = (o * _silu(r_ref[...].astype(f32))).astype(o_ref.dtype)


def _gla(proj, log_a, norm_g, bsz, seq, dk, dv):
    n = proj.shape[0]
    ch = 2 * GLA_CHUNK if seq % (2 * GLA_CHUNK) == 0 else GLA_CHUNK
    nc = seq // ch
    kcols = GLA_HEADS * dk
    vcols = GLA_HEADS * dv
    rowb = lambda b, h, c: b * nc + c
    kern = functools.partial(_gla_kernel, scale=dk ** -0.5)
    return pl.pallas_call(
        kern,
        grid=(bsz, GLA_HEADS, nc),
        in_specs=[
            pl.BlockSpec((ch, dk), lambda b, h, c: (rowb(b, h, c), h)),
            pl.BlockSpec((ch, dk), lambda b, h, c: (rowb(b, h, c), kcols // dk + h)),
            pl.BlockSpec((ch, dv), lambda b, h, c: (rowb(b, h, c), 2 * kcols // dv + h)),
            pl.BlockSpec((ch, dk), lambda b, h, c: (rowb(b, h, c), h)),
            pl.BlockSpec((ch, dv), lambda b, h, c: (rowb(b, h, c), (2 * kcols + vcols) // dv + h)),
            pl.BlockSpec((1, dv), lambda b, h, c: (0, 0)),
        ],
        out_specs=pl.BlockSpec((ch, dv), lambda b, h, c: (rowb(b, h, c), h)),
        out_shape=jax.ShapeDtypeStruct((n, vcols), bf16),
        scratch_shapes=[pltpu.VMEM((dk, dv), f32)],
        compiler_params=_cparams(3),
        name="gla_chunked",
    )(proj, proj, proj, log_a, proj, norm_g.reshape(1, dv))


GATHER_UNROLL = 8


def _slab_copy(src_hbm, dst_vmem, sem, src_row, dst_row):
    return pltpu.make_async_copy(src_hbm.at[pl.ds(src_row, 1)], dst_vmem.at[pl.ds(dst_row, 1)], sem)


def _issue_slabs(idx_ref, n_idx, src_hbm, dst, sem):
    def body(r2, carry):
        for u in range(2):
            r = 2 * r2 + u
            _slab_copy(src_hbm, dst, sem, idx_ref[0, r], r).start(priority=u)
        return carry

    lax.fori_loop(0, n_idx // 2, body, 0, unroll=GATHER_UNROLL // 2)


def _wait_slabs(n_idx, src_hbm, dst, sem):
    def body(r, carry):
        _slab_copy(src_hbm, dst, sem, 0, r).wait()
        return carry

    lax.fori_loop(0, n_idx, body, 0, unroll=GATHER_UNROLL)


def _fetch_slabs(cur_ref, nxt_ref, src_hbm, buf, sem, n_idx):
    s = pl.program_id(0)
    slot = s % 2

    @pl.when(s == 0)
    def _():
        _issue_slabs(cur_ref, n_idx, src_hbm, buf.at[0], sem.at[0])

    @pl.when(s + 1 < pl.num_programs(0))
    def _():
        _issue_slabs(nxt_ref, n_idx, src_hbm, buf.at[1 - slot], sem.at[1 - slot])

    _wait_slabs(n_idx, src_hbm, buf.at[slot], sem.at[slot])
    return slot


def _idx_specs(n_steps, n_idx):
    return [
        pl.BlockSpec((None, 1, n_idx), lambda i: (i, 0, 0), memory_space=pltpu.SMEM),
        pl.BlockSpec((None, 1, n_idx), lambda i: (jnp.minimum(i + 1, n_steps - 1), 0, 0),
                     memory_space=pltpu.SMEM),
    ]


def _gather_kernel(cur_ref, nxt_ref, h_hbm, o_ref, buf, sem, *, tr):
    slot = _fetch_slabs(cur_ref, nxt_ref, h_hbm, buf, sem, tr)
    o_ref[...] = _unpack_bf16_pairs(buf[slot].reshape(tr, o_ref.shape[1] // 2))


def _gather_rows(h, row_token, tr):
    n, nslab, _ = h.shape
    r = row_token.shape[0]
    kern = functools.partial(_gather_kernel, tr=tr)
    idx = row_token.reshape(r // tr, 1, tr)
    return pl.pallas_call(
        kern,
        grid=(r // tr,),
        in_specs=_idx_specs(r // tr, tr) + [pl.BlockSpec(memory_space=pl.ANY)],
        out_specs=pl.BlockSpec((tr, 2 * nslab * LANES), lambda i: (i, 0)),
        out_shape=jax.ShapeDtypeStruct((r, 2 * nslab * LANES), bf16),
        scratch_shapes=[pltpu.VMEM((2, tr, nslab, LANES), jnp.uint32), pltpu.SemaphoreType.DMA((2,))],
        compiler_params=_cparams(1),
        name="moe_gather",
    )(idx, idx, h)


def _combine_kernel(cur_ref, nxt_ref, y_hbm, w_ref, x_ref, gate_ref, fn_ref, o_ref, buf, sem, *, tt):
    slot = _fetch_slabs(cur_ref, nxt_ref, y_hbm, buf, sem, TOP_K * tt)
    ys = buf[slot].reshape(TOP_K * tt, x_ref.shape[1])
    w = w_ref[...]
    moe = w[:, 0:1] * ys[0:tt] + w[:, 1:2] * ys[tt:2 * tt]
    x = x_ref[...] + gate_ref[...] * moe
    y = x * lax.rsqrt(jnp.mean(x * x, axis=-1, keepdims=True) + NORM_EPS)
    o_ref[...] = y * fn_ref[...]


def _combine(y, pos, top_w, x, mod, i_gate, final_norm, seq):
    n, d = x.shape
    nslab = d // LANES
    tt = min(128, seq)
    kern = functools.partial(_combine_kernel, tt=tt)
    idx = pos.reshape(n // tt, tt, TOP_K).transpose(0, 2, 1).reshape(n // tt, 1, TOP_K * tt)
    return pl.pallas_call(
        kern,
        grid=(n // tt,),
        in_specs=_idx_specs(n // tt, TOP_K * tt) + [
            pl.BlockSpec(memory_space=pl.ANY),
            pl.BlockSpec((tt, LANES), lambda i: (i, 0)),
            pl.BlockSpec((tt, d), lambda i: (i, 0)),
            pl.BlockSpec((None, None, 1, d), lambda i: (i * tt // seq, i_gate, 0, 0)),
            pl.BlockSpec((1, d), lambda i: (0, 0)),
        ],
        out_specs=pl.BlockSpec((tt, d), lambda i: (i, 0)),
        out_shape=jax.ShapeDtypeStruct((n, d), f32),
        scratch_shapes=[pltpu.VMEM((2, TOP_K * tt, nslab, LANES), f32),
                        pltpu.SemaphoreType.DMA((2,))],
        compiler_params=_cparams(1),
        name="moe_combine_final_norm",
    )(idx, idx, y, top_w, x, mod, final_norm.reshape(1, d))


def _routing_tables(top_i, bm):
    n = top_i.shape[0]
    e_flat = top_i.reshape(-1)
    onehot = (e_flat[:, None] == jnp.arange(N_EXPERTS)[None, :]).astype(jnp.int32)
    counts = jnp.sum(onehot, axis=0)
    rank = jnp.take_along_axis(jnp.cumsum(onehot, axis=0) - onehot, e_flat[:, None], axis=1)[:, 0]
    padded = ((counts + bm - 1) // bm) * bm
    ends = jnp.cumsum(padded)
    starts = ends - padded
    pos = starts[e_flat] + rank
    n_rows = n * TOP_K + N_EXPERTS * bm
    n_tiles = n_rows // bm
    num_tiles = (ends[-1] // bm).astype(jnp.int32)
    tile_start = jnp.minimum(jnp.arange(n_tiles), num_tiles - 1) * bm
    tile_group = jnp.minimum(jnp.searchsorted(ends, tile_start, side="right"),
                             N_EXPERTS - 1).astype(jnp.int32)
    row_token = jnp.zeros((n_rows,), jnp.int32).at[pos].set(jnp.arange(n * TOP_K, dtype=jnp.int32) // TOP_K)
    return pos.astype(jnp.int32), row_token, tile_group, num_tiles.reshape(1)


def _rope_tables(seq, hd):
    inv = ROPE_THETA ** (-jnp.arange(0, hd, 2, dtype=f32) / hd)
    ang = jnp.arange(seq).astype(f32)[:, None] * inv[None, :]
    cos, sin = jnp.cos(ang), jnp.sin(ang)
    return jnp.concatenate([cos, cos], axis=-1), jnp.concatenate([-sin, sin], axis=-1)


def kernel(x, c, norm_gains, ada_w, ada_b, e_w_in, e_conv_w, e_conv_b, e_conv_ln_g, e_conv_ln_b, e_diff_lambda, e_diff_subln, e_w_out, e_ffn_gate, e_ffn_up, e_ffn_down, o_w_in, o_gate_w2, o_gate_b, o_gla_norm, o_w_out, o_router, o_exp_gate, o_exp_up, o_exp_down, final_norm):
    bsz, seq, d = x.shape
    n = bsz * seq
    depth = ada_w.shape[0]
    assert depth == 2, "trunk is one even (conv + diff-attn) and one odd (GLA + experts) layer"
    xf = x.reshape(n, d)
    mods = _ada(c, ada_w, ada_b)
    bm = min(1024, seq)

    mod = mods[0]
    conv_ch = e_conv_w.shape[2]
    hd = e_diff_lambda.shape[2]
    qk_cols = 2 * DIFF_HEADS * hd
    q_col0 = 2 * conv_ch
    k_col0 = q_col0 + qk_cols
    v_col0 = k_col0 + qk_cols
    even_in = e_w_in.shape[2]
    bn = 512
    h = _normmod(xf, norm_gains[0, 0], mod, 0, 1, seq)
    cos_t, sin_t = _rope_tables(seq, hd)
    bmi = min(512, seq)
    bni = max(t for t in (1024, 512, 256)
              if all(cc % t == 0 for cc in (q_col0, k_col0, v_col0, even_in)))
    rope_spec = pl.BlockSpec(
        (bmi, hd), lambda j, i, te, nt: (jnp.minimum(i, nt[0] - 1) % (seq // bmi), 0))
    ep = functools.partial(_ep_rope, q_tiles=(q_col0 // bni, k_col0 // bni),
                           k_tiles=(k_col0 // bni, v_col0 // bni),
                           q_scale=hd ** -0.5 * math.log2(math.e))
    proj = _gmm([h], [e_w_in], bm=bmi, bn=bni, n_cols=even_in, epilogue=ep,
                extras=(cos_t, sin_t), extra_specs=(rope_spec, rope_spec),
                out_dtype=bf16, name="even_in_proj")
    y_a = _conv(proj, e_conv_w[0], e_conv_b[0], e_conv_ln_g[0], e_conv_ln_b[0], seq)
    lambda_init = 0.8 - 0.6 * math.exp(-0.3 * 0)
    y_b = _diff_attention(proj, e_diff_lambda[0], e_diff_subln[0], bsz, seq,
                          q_col0, k_col0, v_col0, lambda_init)
    xf = _gmm([y_a, y_b], [e_w_out], bm=bm, bn=bn, n_cols=d, epilogue=_ep_resid,
              extras=(xf, mod), extra_specs=(_tile_spec(bm, bn), _gate_spec(bm, bn, seq, 2)),
              out_dtype=f32, name="even_out_proj")
    h = _normmod(xf, norm_gains[0, 1], mod, 3, 4, seq)
    d_ff = e_ffn_gate.shape[2]
    bnf = 256
    act = _gmm([h], [e_ffn_gate, e_ffn_up], bm=min(2048, seq), bn=bnf, n_cols=d_ff,
               epilogue=_ep_swiglu, out_dtype=bf16, name="ffn_gate_up")
    half = d_ff // 2
    bmd = min(1024, seq)
    part = _gmm([act], [e_ffn_down], bm=bmd, bn=bn, n_cols=d, k_block=half, k_index=0,
                epilogue=_ep_plain, out_dtype=f32, name="ffn_down_lo")
    xf = _gmm([act], [e_ffn_down], bm=bmd, bn=bn, n_cols=d, k_block=half, k_index=1,
              epilogue=_ep_resid_partial,
              extras=(xf, mod, part),
              extra_specs=(_tile_spec(bmd, bn), _gate_spec(bmd, bn, seq, 5), _tile_spec(bmd, bn)),
              out_dtype=f32, name="ffn_down_hi")

    mod = mods[1]
    dk = o_gate_w2.shape[2] // GLA_HEADS
    dv = o_gla_norm.shape[1]
    kcols = GLA_HEADS * dk
    vcols = GLA_HEADS * dv
    main_cols = 2 * kcols + 2 * vcols
    h = _normmod(xf, norm_gains[1, 0], mod, 0, 1, seq)
    w_in_t = jnp.swapaxes(o_w_in, 1, 2)
    proj = _gmm([h], [w_in_t], bm=bm, bn=1024, n_cols=main_cols, epilogue=_ep_plain,
                out_dtype=bf16, w_t=True, name="odd_in_proj")
    log_a = _log_decay(h, w_in_t, main_cols, o_gate_w2[0], o_gate_b[0])
    o = _gla(proj, log_a, o_gla_norm[0], bsz, seq, dk, dv)
    xf = _gmm([o], [o_w_out], bm=bm, bn=bn, n_cols=d, epilogue=_ep_resid,
              extras=(xf, mod), extra_specs=(_tile_spec(bm, bn), _gate_spec(bm, bn, seq, 2)),
              out_dtype=f32, name="odd_out_proj")
    h32, top_i, top_w = _normmod_router(xf, norm_gains[1, 1], mod, 3, 4, seq, o_router[0])
    bme = min(512, seq)
    pos, row_token, tile_group, num_tiles = _routing_tables(top_i, bme)
    hs = _gather_rows(h32, row_token, bme)
    d_fe = o_exp_gate.shape[3]
    act = _gmm([hs], [o_exp_gate[0], o_exp_up[0]], bm=bme, bn=512, n_cols=d_fe,
               epilogue=_ep_swiglu, out_dtype=bf16, tile_group=tile_group, num_tiles=num_tiles,
               name="expert_gate_up")
    ye = _gmm([act], [o_exp_down[0]], bm=bme, bn=1024, n_cols=d, epilogue=_ep_slabs,
              out_dtype=f32, tile_group=tile_group, num_tiles=num_tiles, slab_out=True,
              name="expert_down")
    out = _combine(ye, pos, top_w, xf, mod, 5, final_norm, seq)
    return out.reshape(bsz, seq, d)
```

```python
import functools
import math

import jax
import jax.numpy as jnp
from jax import lax
from jax.experimental import pallas as pl
from jax.experimental.pallas import tpu as pltpu

f32 = jnp.float32
bf16 = jnp.bfloat16

N_MOD = 6
NORM_EPS = 1e-6
CONV_WIDTH = 31
CONV_HALO = 32
DIFF_HEADS = 8
DIFF_SUBLN_EPS = 1e-5
ROPE_THETA = 10000.0
GLA_HEADS = 4
GLA_RANK = 16
GLA_TAU = 16.0
GLA_CHUNK = 64
N_EXPERTS = 8
TOP_K = 2
LANES = 128
VMEM_LIMIT_BYTES = 58 * 1024 * 1024


def _cparams(n_axes):
    return pltpu.CompilerParams(
        dimension_semantics=("arbitrary",) * n_axes,
        vmem_limit_bytes=VMEM_LIMIT_BYTES)


def _sigmoid(x):
    return 1.0 / (1.0 + jnp.exp(-x))


def _silu(x):
    return x * _sigmoid(x)


def _ada_kernel(c_ref, w_ref, b_ref, o_ref):
    ca = _silu(c_ref[...]).astype(bf16)
    w = w_ref[...].astype(bf16)
    o_ref[...] = jnp.dot(ca, w, preferred_element_type=f32) + b_ref[...]


def _ada(c, ada_w, ada_b):
    depth, d, n6 = ada_w.shape
    bsz = c.shape[0]
    rows = 8
    cp = jnp.zeros((rows, d), f32).at[:bsz].set(c)
    bn = min(512, n6)
    out = pl.pallas_call(
        _ada_kernel,
        grid=(depth, n6 // bn),
        in_specs=[
            pl.BlockSpec((rows, d), lambda l, j: (0, 0)),
            pl.BlockSpec((None, d, bn), lambda l, j: (l, 0, j)),
            pl.BlockSpec((None, 1, bn), lambda l, j: (l, 0, j)),
        ],
        out_specs=pl.BlockSpec((None, rows, bn), lambda l, j: (l, 0, j)),
        out_shape=jax.ShapeDtypeStruct((depth, rows, n6), f32),
        compiler_params=_cparams(2),
        name="ada_mod",
    )(cp, ada_w, ada_b.reshape(depth, 1, n6))
    return out[:, :bsz].reshape(depth, bsz, N_MOD, 1, d)


def _normmod_value(x_ref, g_ref, sh_ref, sc_ref):
    x = x_ref[...]
    y = x * lax.rsqrt(jnp.mean(x * x, axis=-1, keepdims=True) + NORM_EPS)
    y = y * g_ref[...]
    return y * (1.0 + sc_ref[...]) + sh_ref[...]


def _normmod_kernel(x_ref, g_ref, sh_ref, sc_ref, o_ref):
    o_ref[...] = _normmod_value(x_ref, g_ref, sh_ref, sc_ref).astype(o_ref.dtype)


def _normmod_specs(bm, d, seq, i_shift, i_scale):
    return [
        pl.BlockSpec((bm, d), lambda i: (i, 0)),
        pl.BlockSpec((1, d), lambda i: (0, 0)),
        pl.BlockSpec((None, None, 1, d), lambda i: (i * bm // seq, i_shift, 0, 0)),
        pl.BlockSpec((None, None, 1, d), lambda i: (i * bm // seq, i_scale, 0, 0)),
    ]


def _normmod(x, gain, mod, i_shift, i_scale, seq):
    n, d = x.shape
    bm = min(512, seq)
    return pl.pallas_call(
        _normmod_kernel,
        grid=(n // bm,),
        in_specs=_normmod_specs(bm, d, seq, i_shift, i_scale),
        out_specs=pl.BlockSpec((bm, d), lambda i: (i, 0)),
        out_shape=jax.ShapeDtypeStruct((n, d), bf16),
        compiler_params=_cparams(1),
        name="normmod",
    )(x, gain.reshape(1, d), mod, mod)


def _store_slabs(o_ref, val):
    o_ref[...] = val.reshape(o_ref.shape)


HI16 = 0xFFFF0000


def _pack_bf16_pairs(v):
    half = v.shape[1] // 2
    lo = pltpu.bitcast(v[:, :half].astype(bf16).astype(f32), jnp.uint32) >> 16
    hi = pltpu.bitcast(v[:, half:].astype(bf16).astype(f32), jnp.uint32) & jnp.uint32(HI16)
    return lo | hi


def _unpack_bf16_pairs(w):
    lo = pltpu.bitcast(w << 16, f32).astype(bf16)
    hi = pltpu.bitcast(w & jnp.uint32(HI16), f32).astype(bf16)
    return jnp.concatenate([lo, hi], axis=1)


def _normmod_router_kernel(x_ref, g_ref, sh_ref, sc_ref, wr_ref, h_ref, idx_ref, wgt_ref):
    h = _normmod_value(x_ref, g_ref, sh_ref, sc_ref)
    _store_slabs(h_ref, _pack_bf16_pairs(h))
    wr = wr_ref[...]
    h_hi = h.astype(bf16)
    h_lo = (h - h_hi.astype(f32)).astype(bf16)
    w_hi = wr.astype(bf16)
    w_lo = (wr - w_hi.astype(f32)).astype(bf16)
    logits = (jnp.dot(h_hi, w_hi, preferred_element_type=f32)
              + jnp.dot(h_hi, w_lo, preferred_element_type=f32)
              + jnp.dot(h_lo, w_hi, preferred_element_type=f32))
    col = lax.broadcasted_iota(jnp.int32, logits.shape, 1)
    neg = jnp.float32(-jnp.inf)
    l0 = jnp.where(col < N_EXPERTS, logits, neg)
    m1 = jnp.max(l0, axis=-1, keepdims=True)
    i1 = jnp.min(jnp.where(l0 == m1, col, LANES), axis=-1, keepdims=True)
    l1 = jnp.where(col == i1, neg, l0)
    m2 = jnp.max(l1, axis=-1, keepdims=True)
    i2 = jnp.min(jnp.where(l1 == m2, col, LANES), axis=-1, keepdims=True)
    e = jnp.exp(m2 - m1)
    w1 = 1.0 / (1.0 + e)
    w2 = e / (1.0 + e)
    idx_ref[...] = jnp.where(col == 0, i1, jnp.where(col == 1, i2, 0))
    wgt_ref[...] = jnp.where(col == 0, w1, jnp.where(col == 1, w2, 0.0))


def _normmod_router(x, gain, mod, i_shift, i_scale, seq, w_router):
    n, d = x.shape
    bm = min(256, seq)
    wr = jnp.zeros((d, LANES), f32).at[:, :N_EXPERTS].set(w_router)
    h, idx, wgt = pl.pallas_call(
        _normmod_router_kernel,
        grid=(n // bm,),
        in_specs=_normmod_specs(bm, d, seq, i_shift, i_scale)
        + [pl.BlockSpec((d, LANES), lambda i: (0, 0))],
        out_specs=[
            pl.BlockSpec((bm, d // 2 // LANES, LANES), lambda i: (i, 0, 0)),
            pl.BlockSpec((bm, LANES), lambda i: (i, 0)),
            pl.BlockSpec((bm, LANES), lambda i: (i, 0)),
        ],
        out_shape=[
            jax.ShapeDtypeStruct((n, d // 2 // LANES, LANES), jnp.uint32),
            jax.ShapeDtypeStruct((n, LANES), jnp.int32),
            jax.ShapeDtypeStruct((n, LANES), f32),
        ],
        compiler_params=_cparams(1),
        name="normmod_router",
    )(x, gain.reshape(1, d), mod, mod, wr)
    return h, idx[:, :TOP_K], wgt


def _gmm_kernel(te_ref, nt_ref, *refs, n_x, n_w, n_extra, k_parts, epilogue, w_t, n_tiles, bn, k_row0):
    x_refs = refs[:n_x]
    w_hbm = refs[n_x:n_x + n_w]
    e_refs = refs[n_x + n_w:n_x + n_w + n_extra]
    o_ref = refs[n_x + n_w + n_extra]
    wf_ref, wb_ref, sem = refs[n_x + n_w + n_extra + 1:]
    j = pl.program_id(0)
    i = pl.program_id(1)
    active = i < nt_ref[0]
    prev = te_ref[jnp.maximum(i - 1, 0)]
    changed = jnp.logical_and(active, jnp.logical_or(i == 0, te_ref[i] != prev))
    k = sum(k_parts)

    def w_copy(t, g, jj):
        c0 = pl.multiple_of(jj * bn, bn)
        if w_t:
            src = w_hbm[t].at[g, pl.ds(c0, bn), pl.ds(k_row0, k)]
        else:
            src = w_hbm[t].at[g, pl.ds(k_row0, k), pl.ds(c0, bn)]
        return pltpu.make_async_copy(src, wf_ref.at[t], sem.at[0])

    @pl.when(changed)
    def _():
        run = j * nt_ref[1] + te_ref[2 * n_tiles + i]
        g = te_ref[i]

        @pl.when(run == 0)
        def _():
            for t in range(n_w):
                w_copy(t, g, j).start()

        for t in range(n_w):
            w_copy(t, g, j).wait()
        for t in range(n_w):
            wb_ref[t] = wf_ref[t].astype(bf16)
        nxt = te_ref[n_tiles + i]
        more_groups = nxt >= 0
        g2 = jnp.where(more_groups, te_ref[jnp.maximum(nxt, 0)], te_ref[0])
        j2 = jnp.where(more_groups, j, j + 1)

        @pl.when(j2 < pl.num_programs(0))
        def _():
            for t in range(n_w):
                w_copy(t, g2, j2).start()

    @pl.when(active)
    def _():
        accs = []
        for t in range(n_w):
            acc = None
            off = 0
            for p in range(n_x):
                kp = k_parts[p]
                if w_t:
                    part = lax.dot_general(x_refs[p][...], wb_ref[t, :, off:off + kp],
                                           (((1,), (1,)), ((), ())), preferred_element_type=f32)
                else:
                    part = jnp.dot(x_refs[p][...], wb_ref[t, off:off + kp, :],
                                   preferred_element_type=f32)
                acc = part if acc is None else acc + part
                off += kp
            accs.append(acc)
        epilogue(accs, e_refs, o_ref)

    @pl.when(jnp.logical_not(active))
    def _():
        o_ref[...] = jnp.zeros(o_ref.shape, o_ref.dtype)


def _gmm(xs, ws, *, bm, bn, n_cols, k_block=None, k_index=0, epilogue, extras=(), extra_specs=(),
         out_dtype, tile_group=None, num_tiles=None, slab_out=False, w_t=False, name):
    m = xs[0].shape[0]
    k_parts = tuple(x.shape[1] for x in xs) if k_block is None else (k_block,)
    k = sum(k_parts)
    n_tiles = m // bm
    if tile_group is None:
        tile_group = jnp.zeros((n_tiles,), jnp.int32)
        num_tiles = jnp.full((1,), n_tiles, jnp.int32)
    xk = k_index if k_block is not None else 0
    tid = jnp.arange(n_tiles, dtype=jnp.int32)
    first = jnp.logical_and(tid < num_tiles[0],
                            jnp.concatenate([jnp.ones((1,), bool), tile_group[1:] != tile_group[:-1]]))
    ordinal = jnp.cumsum(first.astype(jnp.int32)) - 1
    first_at = jnp.where(first, tid, n_tiles)
    nxt = jnp.concatenate([lax.cummin(first_at[::-1])[::-1][1:], jnp.full((1,), n_tiles, jnp.int32)])
    nxt = jnp.where(nxt >= n_tiles, -1, nxt).astype(jnp.int32)
    tables = jnp.concatenate([tile_group.astype(jnp.int32), nxt, ordinal.astype(jnp.int32)])
    counts = jnp.stack([num_tiles[0], ordinal[jnp.maximum(num_tiles[0] - 1, 0)] + 1]).astype(jnp.int32)

    def row(i, nt):
        return jnp.minimum(i, nt[0] - 1)

    in_specs = [pl.BlockSpec((bm, kp), lambda j, i, te, nt: (row(i, nt), xk)) for kp in k_parts]
    in_specs += [pl.BlockSpec(memory_space=pl.ANY) for _ in ws]
    in_specs += list(extra_specs)
    w_tile = (bn, k) if w_t else (k, bn)
    kern = functools.partial(_gmm_kernel, n_x=len(xs), n_w=len(ws), n_extra=len(extras),
                             k_parts=k_parts, epilogue=epilogue, w_t=w_t, n_tiles=n_tiles, bn=bn,
                             k_row0=xk * k)
    if slab_out:
        out_spec = pl.BlockSpec((bm, bn // LANES, LANES), lambda j, i, te, nt: (i, j, 0))
        out_shape = jax.ShapeDtypeStruct((m, n_cols // LANES, LANES), out_dtype)
    else:
        out_spec = pl.BlockSpec((bm, bn), lambda j, i, te, nt: (i, j))
        out_shape = jax.ShapeDtypeStruct((m, n_cols), out_dtype)
    return pl.pallas_call(
        kern,
        grid_spec=pltpu.PrefetchScalarGridSpec(
            num_scalar_prefetch=2,
            grid=(n_cols // bn, n_tiles),
            in_specs=in_specs,
            out_specs=out_spec,
            scratch_shapes=[pltpu.VMEM((len(ws),) + w_tile, f32),
                            pltpu.VMEM((len(ws),) + w_tile, bf16),
                            pltpu.SemaphoreType.DMA((1,))],
        ),
        out_shape=out_shape,
        compiler_params=_cparams(2),
        name=name,
    )(tables, counts, *xs, *ws, *extras)


def _ep_plain(accs, e_refs, o_ref):
    o_ref[...] = accs[0].astype(o_ref.dtype)


def _ep_swiglu(accs, e_refs, o_ref):
    g, u = accs
    o_ref[...] = (_silu(g) * u).astype(o_ref.dtype)


def _ep_resid(accs, e_refs, o_ref):
    res_ref, gate_ref = e_refs
    o_ref[...] = res_ref[...] + gate_ref[...] * accs[0]


def _ep_resid_partial(accs, e_refs, o_ref):
    res_ref, gate_ref, part_ref = e_refs
    o_ref[...] = res_ref[...] + gate_ref[...] * (part_ref[...] + accs[0])


def _ep_slabs(accs, e_refs, o_ref):
    _store_slabs(o_ref, accs[0])


def _ep_rope(accs, e_refs, o_ref, *, q_tiles, k_tiles, q_scale):
    cos_ref, sin_ref = e_refs
    acc = accs[0]
    j = pl.program_id(0)
    is_q = jnp.logical_and(j >= q_tiles[0], j < q_tiles[1])
    is_k = jnp.logical_and(j >= k_tiles[0], j < k_tiles[1])

    def roped(scale):
        cos = cos_ref[...]
        sin = sin_ref[...]
        outs = []
        for hh in range(acc.shape[1] // LANES):
            a = acc[:, hh * LANES:(hh + 1) * LANES]
            rot = pltpu.roll(a, LANES // 2, 1)
            outs.append((a * cos + rot * sin) * scale)
        return jnp.concatenate(outs, axis=1)

    @pl.when(is_q)
    def _():
        o_ref[...] = roped(q_scale).astype(o_ref.dtype)

    @pl.when(is_k)
    def _():
        o_ref[...] = roped(1.0).astype(o_ref.dtype)

    @pl.when(jnp.logical_not(jnp.logical_or(is_q, is_k)))
    def _():
        o_ref[...] = acc.astype(o_ref.dtype)


def _gate_spec(bm, bn, seq, i_gate):
    return pl.BlockSpec((None, None, 1, bn),
                        lambda j, i, te, nt: (jnp.minimum(i, nt[0] - 1) * bm // seq, i_gate, 0, j))


def _tile_spec(bm, bn):
    return pl.BlockSpec((bm, bn), lambda j, i, te, nt: (jnp.minimum(i, nt[0] - 1), j))


def _conv_kernel(val_ref, gate_ref, hval_ref, hgate_ref, w_ref, b_ref, lg_ref, lb_ref,
                 o_ref, u_scr, v_scr, *, ts, seq, rows):
    i = pl.program_id(0)
    first = (i * ts) % seq == 0
    hu = hval_ref[...].astype(f32) * _sigmoid(hgate_ref[...].astype(f32))
    u_scr[0:CONV_HALO, :] = jnp.where(first, 0.0, hu)
    u_scr[CONV_HALO:, :] = val_ref[...].astype(f32) * _sigmoid(gate_ref[...].astype(f32))
    lead = CONV_HALO - (CONV_WIDTH - 1)

    cw = LANES
    sub = 8
    wrows = rows + CONV_HALO

    def chunk(c, carry):
        r0 = pl.multiple_of(c * rows, rows)
        for c0 in range(0, u_scr.shape[1], cw):
            win = u_scr[pl.ds(r0, wrows), c0:c0 + cw]
            acc = jnp.zeros((rows, cw), f32)
            for s in range(sub):
                taps = [j for j in range(CONV_WIDTH) if (lead + j) % sub == s]
                if not taps:
                    continue
                shifted = win if s == 0 else pltpu.roll(win, wrows - s, 0)
                for j in taps:
                    a = (lead + j - s)
                    acc = acc + w_ref[j:j + 1, c0:c0 + cw] * shifted[a:a + rows, :]
            v_scr[pl.ds(r0, rows), c0:c0 + cw] = acc
        return carry

    lax.fori_loop(0, ts // rows, chunk, 0)
    u = v_scr[...] + b_ref[...]
    mu = jnp.mean(u, axis=-1, keepdims=True)
    var = jnp.mean(jnp.square(u - mu), axis=-1, keepdims=True)
    y = (u - mu) * lax.rsqrt(var + NORM_EPS) * lg_ref[...] + lb_ref[...]
    o_ref[...] = _silu(y).astype(o_ref.dtype)


def _conv(proj, conv_w, conv_b, ln_g, ln_b, seq):
    n = proj.shape[0]
    ch = conv_w.shape[1]
    ts = min(256, seq)
    hb = ts // CONV_HALO
    halo_row = lambda i: jnp.maximum(i * hb - 1, 0)
    vec = lambda: pl.BlockSpec((1, ch), lambda i: (0, 0))
    kern = functools.partial(_conv_kernel, ts=ts, seq=seq, rows=64)
    return pl.pallas_call(
        kern,
        grid=(n // ts,),
        in_specs=[
            pl.BlockSpec((ts, ch), lambda i: (i, 0)),
            pl.BlockSpec((ts, ch), lambda i: (i, 1)),
            pl.BlockSpec((CONV_HALO, ch), lambda i: (halo_row(i), 0)),
            pl.BlockSpec((CONV_HALO, ch), lambda i: (halo_row(i), 1)),
            pl.BlockSpec((CONV_WIDTH, ch), lambda i: (0, 0)),
            vec(), vec(), vec(),
        ],
        out_specs=pl.BlockSpec((ts, ch), lambda i: (i, 0)),
        out_shape=jax.ShapeDtypeStruct((n, ch), bf16),
        scratch_shapes=[pltpu.VMEM((ts + CONV_HALO, ch), f32), pltpu.VMEM((ts, ch), f32)],
        compiler_params=_cparams(1),
        name="conformer_conv",
    )(proj, proj, proj, proj, conv_w, conv_b.reshape(1, ch), ln_g.reshape(1, ch), ln_b.reshape(1, ch))


def _attn_kernel(qt_ref, kt_ref, q_ref, k_ref, v_ref, lam_ref, g_ref, o_ref, m_scr, l_scr, acc_scr,
                 *, lambda_init, hd):
    t = pl.program_id(2)
    qi = qt_ref[t]
    ki = kt_ref[t]
    tq = q_ref.shape[0]
    tk = k_ref.shape[0]
    last = ki == (qi + 1) * (tq // tk) - 1
    crosses = (ki + 1) * tk - 1 > qi * tq

    @pl.when(ki == 0)
    def _():
        m_scr[...] = jnp.full(m_scr.shape, -jnp.inf, f32)
        l_scr[...] = jnp.zeros(l_scr.shape, f32)
        acc_scr[...] = jnp.zeros(acc_scr.shape, f32)

    def lane_fold(x, op):
        out = x[:, 0:LANES]
        for cb in range(1, x.shape[1] // LANES):
            out = op(out, x[:, cb * LANES:(cb + 1) * LANES])
        return out

    def step(masked):
        v = v_ref[...]
        for s in range(2):
            q = q_ref[:, s * hd:(s + 1) * hd]
            k = k_ref[:, s * hd:(s + 1) * hd]
            sc = lax.dot_general(q, k, (((1,), (1,)), ((), ())), preferred_element_type=f32)
            if masked:
                row = lax.broadcasted_iota(jnp.int32, sc.shape, 0)
                col = lax.broadcasted_iota(jnp.int32, sc.shape, 1)
                sc = jnp.where(col + (ki * tk - qi * tq) <= row, sc, -jnp.inf)
            m_cur = jnp.max(lane_fold(sc, jnp.maximum), axis=-1, keepdims=True)
            m_prev = m_scr[s]
            m_new = jnp.maximum(m_prev, m_cur)
            alpha = jnp.exp2(m_prev - m_new)
            p = jnp.exp2(sc - jnp.tile(m_new, (1, sc.shape[1] // LANES)))
            l_scr[s] = alpha * l_scr[s] + lane_fold(p, jnp.add)
            acc_scr[s] = (jnp.tile(alpha, (1, v.shape[1] // LANES)) * acc_scr[s]
                          + jnp.dot(p.astype(bf16), v, preferred_element_type=f32))
            m_scr[s] = m_new

    @pl.when(jnp.logical_not(crosses))
    def _():
        step(False)

    @pl.when(crosses)
    def _():
        step(True)

    @pl.when(last)
    def _():
        lp = lam_ref[...]
        lam = (jnp.exp(jnp.sum(lp[0:1] * lp[1:2], axis=-1, keepdims=True))
               - jnp.exp(jnp.sum(lp[2:3] * lp[3:4], axis=-1, keepdims=True)) + lambda_init)
        l0 = jnp.sum(l_scr[0], axis=-1, keepdims=True)
        l1 = jnp.sum(l_scr[1], axis=-1, keepdims=True)
        o = acc_scr[0] / l0 - lam * (acc_scr[1] / l1)
        o = o * lax.rsqrt(jnp.mean(o * o, axis=-1, keepdims=True) + DIFF_SUBLN_EPS)
        o_ref[...] = ((o * g_ref[...]) * (1.0 - lambda_init)).astype(o_ref.dtype)


def _diff_attention(proj, lam_p, subln_g, bsz, seq, q_col0, k_col0, v_col0, lambda_init):
    n = proj.shape[0]
    hd = lam_p.shape[1]
    vd = 2 * hd
    tq = min(1024, seq)
    tk = min(512, seq)
    nq = seq // tq
    nk = seq // tk
    pairs = [(qi, ki) for qi in range(nq) for ki in range((qi + 1) * (tq // tk))]
    q_of = jnp.asarray([p[0] for p in pairs], jnp.int32)
    k_of = jnp.asarray([p[1] for p in pairs], jnp.int32)
    kern = functools.partial(_attn_kernel, lambda_init=lambda_init, hd=hd)
    return pl.pallas_call(
        kern,
        grid_spec=pltpu.PrefetchScalarGridSpec(
            num_scalar_prefetch=2,
            grid=(bsz, DIFF_HEADS, len(pairs)),
            in_specs=[
                pl.BlockSpec((tq, vd), lambda b, h, t, qt, kt: (b * nq + qt[t], q_col0 // vd + h)),
                pl.BlockSpec((tk, vd), lambda b, h, t, qt, kt: (b * nk + kt[t], k_col0 // vd + h)),
                pl.BlockSpec((tk, vd), lambda b, h, t, qt, kt: (b * nk + kt[t], v_col0 // vd + h)),
                pl.BlockSpec((4, hd), lambda b, h, t, qt, kt: (0, 0)),
                pl.BlockSpec((1, vd), lambda b, h, t, qt, kt: (0, 0)),
            ],
            out_specs=pl.BlockSpec((tq, vd), lambda b, h, t, qt, kt: (b * nq + qt[t], h)),
            scratch_shapes=[pltpu.VMEM((2, tq, LANES), f32), pltpu.VMEM((2, tq, LANES), f32),
                            pltpu.VMEM((2, tq, vd), f32)],
        ),
        out_shape=jax.ShapeDtypeStruct((n, DIFF_HEADS * vd), bf16),
        compiler_params=_cparams(3),
        name="diff_attention",
    )(q_of, k_of, proj, proj, proj, lam_p, subln_g.reshape(1, vd))


def _loga_kernel(h_ref, w1_ref, w2_ref, b_ref, o_ref):
    rowi = lax.broadcasted_iota(jnp.int32, w1_ref.shape, 0)
    w1 = jnp.where(rowi < GLA_RANK, w1_ref[...], 0.0).astype(bf16)
    g1 = lax.dot_general(h_ref[...], w1, (((1,), (1,)), ((), ())), preferred_element_type=f32)
    gpre = jnp.dot(g1.astype(bf16), w2_ref[...].astype(bf16), preferred_element_type=f32) + b_ref[...]
    nx = -gpre
    softplus = jnp.maximum(nx, 0.0) + jnp.log1p(jnp.exp(-jnp.abs(nx)))
    o_ref[...] = -softplus / GLA_TAU


def _log_decay(h, w_in_t, col0, w2, bias):
    n, d = h.shape
    kdim = w2.shape[1]
    bm = min(512, n)
    w2p = jnp.zeros((LANES, kdim), f32).at[:GLA_RANK].set(w2)
    return pl.pallas_call(
        _loga_kernel,
        grid=(n // bm,),
        in_specs=[
            pl.BlockSpec((bm, d), lambda i: (i, 0)),
            pl.BlockSpec((None, LANES, d), lambda i: (0, col0 // LANES, 0)),
            pl.BlockSpec((LANES, kdim), lambda i: (0, 0)),
            pl.BlockSpec((1, kdim), lambda i: (0, 0)),
        ],
        out_specs=pl.BlockSpec((bm, kdim), lambda i: (i, 0)),
        out_shape=jax.ShapeDtypeStruct((n, kdim), f32),
        compiler_params=_cparams(1),
        name="gla_log_decay",
    )(h, w_in_t, w2p, bias.reshape(1, kdim))


def _gla_kernel(q_ref, k_ref, v_ref, la_ref, r_ref, g_ref, o_ref, st_ref, *, scale):
    c = pl.program_id(2)

    @pl.when(c == 0)
    def _():
        st_ref[...] = jnp.zeros(st_ref.shape, f32)

    ch = q_ref.shape[0]
    dv = v_ref.shape[1]
    la = la_ref[...]
    row = lax.broadcasted_iota(jnp.int32, (ch, ch), 0)
    col = lax.broadcasted_iota(jnp.int32, (ch, ch), 1)
    causal = row >= col
    dk = la.shape[1]
    la_hi = la.astype(bf16)
    la_lo = (la - la_hi.astype(f32)).astype(bf16)
    la2 = jnp.concatenate([la_hi, la_lo], axis=1)
    b2 = jnp.dot(causal.astype(bf16), la2, preferred_element_type=f32)
    b = b2[:, :dk] + b2[:, dk:]
    b_last = b[ch - 1:ch, :]
    b_mid = b[ch // 2 - 1:ch // 2, :]
    s2 = lax.dot_general(la2, jnp.ones((ch, LANES), bf16), (((0,), (0,)), ((), ())),
                         preferred_element_type=f32)
    b_last_col = s2[:dk] + s2[dk:]
    q = q_ref[...].astype(f32) * scale
    k = k_ref[...].astype(f32)
    q_t = (q * jnp.exp(b)).astype(bf16)
    q_rel = (q * jnp.exp(b - b_mid)).astype(bf16)
    k_rel = (k * jnp.exp(b_mid - b)).astype(bf16)
    k_dec = (k * jnp.exp(b_last - b)).astype(bf16)
    v = v_ref[...]
    attn = lax.dot_general(q_rel, k_rel, (((1,), (1,)), ((), ())), preferred_element_type=f32)
    attn = jnp.where(causal, attn, 0.0)
    st = st_ref[...]
    o = (jnp.dot(attn.astype(bf16), v, preferred_element_type=f32)
         + jnp.dot(q_t, st.astype(bf16), preferred_element_type=f32))
    kv = lax.dot_general(k_dec, v, (((0,), (0,)), ((), ())), preferred_element_type=f32)
    st_ref[...] = jnp.tile(jnp.exp(b_last_col), (1, dv // LANES)) * st + kv
    o = o * lax.rsqrt(jnp.mean(o * o, axis=-1, keepdims=True) + NORM_EPS) * g_ref[...]
    o_ref[...] = (o * _silu(r_ref[...].astype(f32))).astype(o_ref.dtype)


def _gla(proj, log_a, norm_g, bsz, seq, dk, dv):
    n = proj.shape[0]
    ch = 2 * GLA_CHUNK if seq % (2 * GLA_CHUNK) == 0 else GLA_CHUNK
    nc = seq // ch
    kcols = GLA_HEADS * dk
    vcols = GLA_HEADS * dv
    rowb = lambda b, h, c: b * nc + c
    kern = functools.partial(_gla_kernel, scale=dk ** -0.5)
    return pl.pallas_call(
        kern,
        grid=(bsz, GLA_HEADS, nc),
        in_specs=[
            pl.BlockSpec((ch, dk), lambda b, h, c: (rowb(b, h, c), h)),
            pl.BlockSpec((ch, dk), lambda b, h, c: (rowb(b, h, c), kcols // dk + h)),
            pl.BlockSpec((ch, dv), lambda b, h, c: (rowb(b, h, c), 2 * kcols // dv + h)),
            pl.BlockSpec((ch, dk), lambda b, h, c: (rowb(b, h, c), h)),
            pl.BlockSpec((ch, dv), lambda b, h, c: (rowb(b, h, c), (2 * kcols + vcols) // dv + h)),
            pl.BlockSpec((1, dv), lambda b, h, c: (0, 0)),
        ],
        out_specs=pl.BlockSpec((ch, dv), lambda b, h, c: (rowb(b, h, c), h)),
        out_shape=jax.ShapeDtypeStruct((n, vcols), bf16),
        scratch_shapes=[pltpu.VMEM((dk, dv), f32)],
        compiler_params=_cparams(3),
        name="gla_chunked",
    )(proj, proj, proj, log_a, proj, norm_g.reshape(1, dv))


GATHER_UNROLL = 8


def _slab_copy(src_hbm, dst_vmem, sem, src_row, dst_row):
    return pltpu.make_async_copy(src_hbm.at[pl.ds(src_row, 1)], dst_vmem.at[pl.ds(dst_row, 1)], sem)


def _issue_slabs(idx_ref, n_idx, src_hbm, dst, sem):
    def body(r2, carry):
        for u in range(2):
            r = 2 * r2 + u
            _slab_copy(src_hbm, dst, sem, idx_ref[0, r], r).start(priority=u)
        return carry

    lax.fori_loop(0, n_idx // 2, body, 0, unroll=GATHER_UNROLL // 2)


def _wait_slabs(n_idx, src_hbm, dst, sem):
    def body(r, carry):
        _slab_copy(src_hbm, dst, sem, 0, r).wait()
        return carry

    lax.fori_loop(0, n_idx, body, 0, unroll=GATHER_UNROLL)


def _fetch_slabs(cur_ref, nxt_ref, src_hbm, buf, sem, n_idx):
    s = pl.program_id(0)
    slot = s % 2

    @pl.when(s == 0)
    def _():
        _issue_slabs(cur_ref, n_idx, src_hbm, buf.at[0], sem.at[0])

    @pl.when(s + 1 < pl.num_programs(0))
    def _():
        _issue_slabs(nxt_ref, n_idx, src_hbm, buf.at[1 - slot], sem.at[1 - slot])

    _wait_slabs(n_idx, src_hbm, buf.at[slot], sem.at[slot])
    return slot


def _idx_specs(n_steps, n_idx):
    return [
        pl.BlockSpec((None, 1, n_idx), lambda i: (i, 0, 0), memory_space=pltpu.SMEM),
        pl.BlockSpec((None, 1, n_idx), lambda i: (jnp.minimum(i + 1, n_steps - 1), 0, 0),
                     memory_space=pltpu.SMEM),
    ]


def _gather_kernel(cur_ref, nxt_ref, h_hbm, o_ref, buf, sem, *, tr):
    slot = _fetch_slabs(cur_ref, nxt_ref, h_hbm, buf, sem, tr)
    o_ref[...] = _unpack_bf16_pairs(buf[slot].reshape(tr, o_ref.shape[1] // 2))


def _gather_rows(h, row_token, tr):
    n, nslab, _ = h.shape
    r = row_token.shape[0]
    kern = functools.partial(_gather_kernel, tr=tr)
    idx = row_token.reshape(r // tr, 1, tr)
    return pl.pallas_call(
        kern,
        grid=(r // tr,),
        in_specs=_idx_specs(r // tr, tr) + [pl.BlockSpec(memory_space=pl.ANY)],
        out_specs=pl.BlockSpec((tr, 2 * nslab * LANES), lambda i: (i, 0)),
        out_shape=jax.ShapeDtypeStruct((r, 2 * nslab * LANES), bf16),
        scratch_shapes=[pltpu.VMEM((2, tr, nslab, LANES), jnp.uint32), pltpu.SemaphoreType.DMA((2,))],
        compiler_params=_cparams(1),
        name="moe_gather",
    )(idx, idx, h)


def _combine_kernel(cur_ref, nxt_ref, y_hbm, w_ref, x_ref, gate_ref, fn_ref, o_ref, buf, sem, *, tt):
    slot = _fetch_slabs(cur_ref, nxt_ref, y_hbm, buf, sem, TOP_K * tt)
    ys = buf[slot].reshape(TOP_K * tt, x_ref.shape[1])
    w = w_ref[...]
    moe = w[:, 0:1] * ys[0:tt] + w[:, 1:2] * ys[tt:2 * tt]
    x = x_ref[...] + gate_ref[...] * moe
    y = x * lax.rsqrt(jnp.mean(x * x, axis=-1, keepdims=True) + NORM_EPS)
    o_ref[...] = y * fn_ref[...]


def _combine(y, pos, top_w, x, mod, i_gate, final_norm, seq):
    n, d = x.shape
    nslab = d // LANES
    tt = min(128, seq)
    kern = functools.partial(_combine_kernel, tt=tt)
    idx = pos.reshape(n // tt, tt, TOP_K).transpose(0, 2, 1).reshape(n // tt, 1, TOP_K * tt)
    return pl.pallas_call(
        kern,
        grid=(n // tt,),
        in_specs=_idx_specs(n // tt, TOP_K * tt) + [
            pl.BlockSpec(memory_space=pl.ANY),
            pl.BlockSpec((tt, LANES), lambda i: (i, 0)),
            pl.BlockSpec((tt, d), lambda i: (i, 0)),
            pl.BlockSpec((None, None, 1, d), lambda i: (i * tt // seq, i_gate, 0, 0)),
            pl.BlockSpec((1, d), lambda i: (0, 0)),
        ],
        out_specs=pl.BlockSpec((tt, d), lambda i: (i, 0)),
        out_shape=jax.ShapeDtypeStruct((n, d), f32),
        scratch_shapes=[pltpu.VMEM((2, TOP_K * tt, nslab, LANES), f32),
                        pltpu.SemaphoreType.DMA((2,))],
        compiler_params=_cparams(1),
        name="moe_combine_final_norm",
    )(idx, idx, y, top_w, x, mod, final_norm.reshape(1, d))


def _routing_tables(top_i, bm):
    n = top_i.shape[0]
    e_flat = top_i.reshape(-1)
    onehot = (e_flat[:, None] == jnp.arange(N_EXPERTS)[None, :]).astype(jnp.int32)
    counts = jnp.sum(onehot, axis=0)
    rank = jnp.take_along_axis(jnp.cumsum(onehot, axis=0) - onehot, e_flat[:, None], axis=1)[:, 0]
    padded = ((counts + bm - 1) // bm) * bm
    ends = jnp.cumsum(padded)
    starts = ends - padded
    pos = starts[e_flat] + rank
    n_rows = n * TOP_K + N_EXPERTS * bm
    n_tiles = n_rows // bm
    num_tiles = (ends[-1] // bm).astype(jnp.int32)
    tile_start = jnp.minimum(jnp.arange(n_tiles), num_tiles - 1) * bm
    tile_group = jnp.minimum(jnp.searchsorted(ends, tile_start, side="right"),
                             N_EXPERTS - 1).astype(jnp.int32)
    row_token = jnp.zeros((n_rows,), jnp.int32).at[pos].set(jnp.arange(n * TOP_K, dtype=jnp.int32) // TOP_K)
    return pos.astype(jnp.int32), row_token, tile_group, num_tiles.reshape(1)


def _rope_tables(seq, hd):
    inv = ROPE_THETA ** (-jnp.arange(0, hd, 2, dtype=f32) / hd)
    ang = jnp.arange(seq).astype(f32)[:, None] * inv[None, :]
    cos, sin = jnp.cos(ang), jnp.sin(ang)
    return jnp.concatenate([cos, cos], axis=-1), jnp.concatenate([-sin, sin], axis=-1)


def kernel(x, c, norm_gains, ada_w, ada_b, e_w_in, e_conv_w, e_conv_b, e_conv_ln_g, e_conv_ln_b, e_diff_lambda, e_diff_subln, e_w_out, e_ffn_gate, e_ffn_up, e_ffn_down, o_w_in, o_gate_w2, o_gate_b, o_gla_norm, o_w_out, o_router, o_exp_gate, o_exp_up, o_exp_down, final_norm):
    bsz, seq, d = x.shape
    n = bsz * seq
    depth = ada_w.shape[0]
    assert depth == 2, "trunk is one even (conv + diff-attn) and one odd (GLA + experts) layer"
    xf = x.reshape(n, d)
    mods = _ada(c, ada_w, ada_b)
    bm = min(1024, seq)

    mod = mods[0]
    conv_ch = e_conv_w.shape[2]
    hd = e_diff_lambda.shape[2]
    qk_cols = 2 * DIFF_HEADS * hd
    q_col0 = 2 * conv_ch
    k_col0 = q_col0 + qk_cols
    v_col0 = k_col0 + qk_cols
    even_in = e_w_in.shape[2]
    bn = 512
    h = _normmod(xf, norm_gains[0, 0], mod, 0, 1, seq)
    cos_t, sin_t = _rope_tables(seq, hd)
    bmi = min(512, seq)
    bni = max(t for t in (1024, 512, 256)
              if all(cc % t == 0 for cc in (q_col0, k_col0, v_col0, even_in)))
    rope_spec = pl.BlockSpec(
        (bmi, hd), lambda j, i, te, nt: (jnp.minimum(i, nt[0] - 1) % (seq // bmi), 0))
    ep = functools.partial(_ep_rope, q_tiles=(q_col0 // bni, k_col0 // bni),
                           k_tiles=(k_col0 // bni, v_col0 // bni),
                           q_scale=hd ** -0.5 * math.log2(math.e))
    proj = _gmm([h], [e_w_in], bm=bmi, bn=bni, n_cols=even_in, epilogue=ep,
                extras=(cos_t, sin_t), extra_specs=(rope_spec, rope_spec),
                out_dtype=bf16, name="even_in_proj")
    y_a = _conv(proj, e_conv_w[0], e_conv_b[0], e_conv_ln_g[0], e_conv_ln_b[0], seq)
    lambda_init = 0.8 - 0.6 * math.exp(-0.3 * 0)
    y_b = _diff_attention(proj, e_diff_lambda[0], e_diff_subln[0], bsz, seq,
                          q_col0, k_col0, v_col0, lambda_init)
    xf = _gmm([y_a, y_b], [e_w_out], bm=bm, bn=bn, n_cols=d, epilogue=_ep_resid,
              extras=(xf, mod), extra_specs=(_tile_spec(bm, bn), _gate_spec(bm, bn, seq, 2)),
              out_dtype=f32, name="even_out_proj")
    h = _normmod(xf, norm_gains[0, 1], mod, 3, 4, seq)
    d_ff = e_ffn_gate.shape[2]
    bnf = 256
    act = _gmm([h], [e_ffn_gate, e_ffn_up], bm=min(2048, seq), bn=bnf, n_cols=d_ff,
               epilogue=_ep_swiglu, out_dtype=bf16, name="ffn_gate_up")
    half = d_ff // 2
    bmd = min(1024, seq)
    part = _gmm([act], [e_ffn_down], bm=bmd, bn=bn, n_cols=d, k_block=half, k_index=0,
                epilogue=_ep_plain, out_dtype=f32, name="ffn_down_lo")
    xf = _gmm([act], [e_ffn_down], bm=bmd, bn=bn, n_cols=d, k_block=half, k_index=1,
              epilogue=_ep_resid_partial,
              extras=(xf, mod, part),
              extra_specs=(_tile_spec(bmd, bn), _gate_spec(bmd, bn, seq, 5), _tile_spec(bmd, bn)),
              out_dtype=f32, name="ffn_down_hi")

    mod = mods[1]
    dk = o_gate_w2.shape[2] // GLA_HEADS
    dv = o_gla_norm.shape[1]
    kcols = GLA_HEADS * dk
    vcols = GLA_HEADS * dv
    main_cols = 2 * kcols + 2 * vcols
    h = _normmod(xf, norm_gains[1, 0], mod, 0, 1, seq)
    w_in_t = jnp.swapaxes(o_w_in, 1, 2)
    proj = _gmm([h], [w_in_t], bm=bm, bn=1024, n_cols=main_cols, epilogue=_ep_plain,
                out_dtype=bf16, w_t=True, name="odd_in_proj")
    log_a = _log_decay(h, w_in_t, main_cols, o_gate_w2[0], o_gate_b[0])
    o = _gla(proj, log_a, o_gla_norm[0], bsz, seq, dk, dv)
    xf = _gmm([o], [o_w_out], bm=bm, bn=bn, n_cols=d, epilogue=_ep_resid,
              extras=(xf, mod), extra_specs=(_tile_spec(bm, bn), _gate_spec(bm, bn, seq, 2)),
              out_dtype=f32, name="odd_out_proj")
    h32, top_i, top_w = _normmod_router(xf, norm_gains[1, 1], mod, 3, 4, seq, o_router[0])
    bme = min(256, seq)
    pos, row_token, tile_group, num_tiles = _routing_tables(top_i, bme)
    hs = _gather_rows(h32, row_token, bme)
    d_fe = o_exp_gate.shape[3]
    act = _gmm([hs], [o_exp_gate[0], o_exp_up[0]], bm=bme, bn=512, n_cols=d_fe,
               epilogue=_ep_swiglu, out_dtype=bf16, tile_group=tile_group, num_tiles=num_tiles,
               name="expert_gate_up")
    ye = _gmm([act], [o_exp_down[0]], bm=bme, bn=1024, n_cols=d, epilogue=_ep_slabs,
              out_dtype=f32, tile_group=tile_group, num_tiles=num_tiles, slab_out=True,
              name="expert_down")
    out = _combine(ye, pos, top_w, xf, mod, 5, final_norm, seq)
    return out.reshape(bsz, seq, d)
```

```python
import functools
import math

import jax
import jax.numpy as jnp
from jax import lax
from jax.experimental import pallas as pl
from jax.experimental.pallas import tpu as pltpu

f32 = jnp.float32
bf16 = jnp.bfloat16

N_MOD = 6
NORM_EPS = 1e-6
CONV_WIDTH = 31
CONV_HALO = 32
DIFF_HEADS = 8
DIFF_SUBLN_EPS = 1e-5
ROPE_THETA = 10000.0
GLA_HEADS = 4
GLA_RANK = 16
GLA_TAU = 16.0
GLA_CHUNK = 64
N_EXPERTS = 8
TOP_K = 2
LANES = 128
VMEM_LIMIT_BYTES = 58 * 1024 * 1024


def _cparams(n_axes):
    return pltpu.CompilerParams(
        dimension_semantics=("arbitrary",) * n_axes,
        vmem_limit_bytes=VMEM_LIMIT_BYTES)


def _sigmoid(x):
    return 1.0 / (1.0 + jnp.exp(-x))


def _silu(x):
    return x * _sigmoid(x)


def _ada_kernel(c_ref, w_ref, b_ref, o_ref):
    ca = _silu(c_ref[...]).astype(bf16)
    w = w_ref[...].astype(bf16)
    o_ref[...] = jnp.dot(ca, w, preferred_element_type=f32) + b_ref[...]


def _ada(c, ada_w, ada_b):
    depth, d, n6 = ada_w.shape
    bsz = c.shape[0]
    rows = 8
    cp = jnp.zeros((rows, d), f32).at[:bsz].set(c)
    bn = min(512, n6)
    out = pl.pallas_call(
        _ada_kernel,
        grid=(depth, n6 // bn),
        in_specs=[
            pl.BlockSpec((rows, d), lambda l, j: (0, 0)),
            pl.BlockSpec((None, d, bn), lambda l, j: (l, 0, j)),
            pl.BlockSpec((None, 1, bn), lambda l, j: (l, 0, j)),
        ],
        out_specs=pl.BlockSpec((None, rows, bn), lambda l, j: (l, 0, j)),
        out_shape=jax.ShapeDtypeStruct((depth, rows, n6), f32),
        compiler_params=_cparams(2),
        name="ada_mod",
    )(cp, ada_w, ada_b.reshape(depth, 1, n6))
    return out[:, :bsz].reshape(depth, bsz, N_MOD, 1, d)


def _normmod_value(x_ref, g_ref, sh_ref, sc_ref):
    x = x_ref[...]
    y = x * lax.rsqrt(jnp.mean(x * x, axis=-1, keepdims=True) + NORM_EPS)
    y = y * g_ref[...]
    return y * (1.0 + sc_ref[...]) + sh_ref[...]


def _normmod_kernel(x_ref, g_ref, sh_ref, sc_ref, o_ref):
    o_ref[...] = _normmod_value(x_ref, g_ref, sh_ref, sc_ref).astype(o_ref.dtype)


def _normmod_specs(bm, d, seq, i_shift, i_scale):
    return [
        pl.BlockSpec((bm, d), lambda i: (i, 0)),
        pl.BlockSpec((1, d), lambda i: (0, 0)),
        pl.BlockSpec((None, None, 1, d), lambda i: (i * bm // seq, i_shift, 0, 0)),
        pl.BlockSpec((None, None, 1, d), lambda i: (i * bm // seq, i_scale, 0, 0)),
    ]


def _normmod(x, gain, mod, i_shift, i_scale, seq):
    n, d = x.shape
    bm = min(512, seq)
    return pl.pallas_call(
        _normmod_kernel,
        grid=(n // bm,),
        in_specs=_normmod_specs(bm, d, seq, i_shift, i_scale),
        out_specs=pl.BlockSpec((bm, d), lambda i: (i, 0)),
        out_shape=jax.ShapeDtypeStruct((n, d), bf16),
        compiler_params=_cparams(1),
        name="normmod",
    )(x, gain.reshape(1, d), mod, mod)


def _store_slabs(o_ref, val):
    o_ref[...] = val.reshape(o_ref.shape)


HI16 = 0xFFFF0000


def _pack_bf16_pairs(v):
    half = v.shape[1] // 2
    lo = pltpu.bitcast(v[:, :half].astype(bf16).astype(f32), jnp.uint32) >> 16
    hi = pltpu.bitcast(v[:, half:].astype(bf16).astype(f32), jnp.uint32) & jnp.uint32(HI16)
    return lo | hi


def _unpack_bf16_pairs(w):
    lo = pltpu.bitcast(w << 16, f32).astype(bf16)
    hi = pltpu.bitcast(w & jnp.uint32(HI16), f32).astype(bf16)
    return jnp.concatenate([lo, hi], axis=1)


def _normmod_router_kernel(x_ref, g_ref, sh_ref, sc_ref, wr_ref, h_ref, idx_ref, wgt_ref):
    h = _normmod_value(x_ref, g_ref, sh_ref, sc_ref)
    _store_slabs(h_ref, _pack_bf16_pairs(h))
    wr = wr_ref[...]
    h_hi = h.astype(bf16)
    h_lo = (h - h_hi.astype(f32)).astype(bf16)
    w_hi = wr.astype(bf16)
    w_lo = (wr - w_hi.astype(f32)).astype(bf16)
    logits = (jnp.dot(h_hi, w_hi, preferred_element_type=f32)
              + jnp.dot(h_hi, w_lo, preferred_element_type=f32)
              + jnp.dot(h_lo, w_hi, preferred_element_type=f32))
    col = lax.broadcasted_iota(jnp.int32, logits.shape, 1)
    neg = jnp.float32(-jnp.inf)
    l0 = jnp.where(col < N_EXPERTS, logits, neg)
    m1 = jnp.max(l0, axis=-1, keepdims=True)
    i1 = jnp.min(jnp.where(l0 == m1, col, LANES), axis=-1, keepdims=True)
    l1 = jnp.where(col == i1, neg, l0)
    m2 = jnp.max(l1, axis=-1, keepdims=True)
    i2 = jnp.min(jnp.where(l1 == m2, col, LANES), axis=-1, keepdims=True)
    e = jnp.exp(m2 - m1)
    w1 = 1.0 / (1.0 + e)
    w2 = e / (1.0 + e)
    idx_ref[...] = jnp.where(col == 0, i1, jnp.where(col == 1, i2, 0))
    wgt_ref[...] = jnp.where(col == 0, w1, jnp.where(col == 1, w2, 0.0))


def _normmod_router(x, gain, mod, i_shift, i_scale, seq, w_router):
    n, d = x.shape
    bm = min(256, seq)
    wr = jnp.zeros((d, LANES), f32).at[:, :N_EXPERTS].set(w_router)
    h, idx, wgt = pl.pallas_call(
        _normmod_router_kernel,
        grid=(n // bm,),
        in_specs=_normmod_specs(bm, d, seq, i_shift, i_scale)
        + [pl.BlockSpec((d, LANES), lambda i: (0, 0))],
        out_specs=[
            pl.BlockSpec((bm, d // 2 // LANES, LANES), lambda i: (i, 0, 0)),
            pl.BlockSpec((bm, LANES), lambda i: (i, 0)),
            pl.BlockSpec((bm, LANES), lambda i: (i, 0)),
        ],
        out_shape=[
            jax.ShapeDtypeStruct((n, d // 2 // LANES, LANES), jnp.uint32),
            jax.ShapeDtypeStruct((n, LANES), jnp.int32),
            jax.ShapeDtypeStruct((n, LANES), f32),
        ],
        compiler_params=_cparams(1),
        name="normmod_router",
    )(x, gain.reshape(1, d), mod, mod, wr)
    return h, idx[:, :TOP_K], wgt


def _gmm_kernel(te_ref, nt_ref, *refs, n_x, n_w, n_extra, k_parts, epilogue, w_t, n_tiles, bn, k_row0):
    x_refs = refs[:n_x]
    w_hbm = refs[n_x:n_x + n_w]
    e_refs = refs[n_x + n_w:n_x + n_w + n_extra]
    o_ref = refs[n_x + n_w + n_extra]
    wf_ref, wb_ref, sem = refs[n_x + n_w + n_extra + 1:]
    j = pl.program_id(0)
    i = pl.program_id(1)
    active = i < nt_ref[0]
    prev = te_ref[jnp.maximum(i - 1, 0)]
    changed = jnp.logical_and(active, jnp.logical_or(i == 0, te_ref[i] != prev))
    k = sum(k_parts)

    def w_copy(t, g, jj):
        c0 = pl.multiple_of(jj * bn, bn)
        if w_t:
            src = w_hbm[t].at[g, pl.ds(c0, bn), pl.ds(k_row0, k)]
        else:
            src = w_hbm[t].at[g, pl.ds(k_row0, k), pl.ds(c0, bn)]
        return pltpu.make_async_copy(src, wf_ref.at[t], sem.at[0])

    @pl.when(changed)
    def _():
        run = j * nt_ref[1] + te_ref[2 * n_tiles + i]
        g = te_ref[i]

        @pl.when(run == 0)
        def _():
            for t in range(n_w):
                w_copy(t, g, j).start()

        for t in range(n_w):
            w_copy(t, g, j).wait()
        for t in range(n_w):
            wb_ref[t] = wf_ref[t].astype(bf16)
        nxt = te_ref[n_tiles + i]
        more_groups = nxt >= 0
        g2 = jnp.where(more_groups, te_ref[jnp.maximum(nxt, 0)], te_ref[0])
        j2 = jnp.where(more_groups, j, j + 1)

        @pl.when(j2 < pl.num_programs(0))
        def _():
            for t in range(n_w):
                w_copy(t, g2, j2).start()

    @pl.when(active)
    def _():
        accs = []
        for t in range(n_w):
            acc = None
            off = 0
            for p in range(n_x):
                kp = k_parts[p]
                if w_t:
                    part = lax.dot_general(x_refs[p][...], wb_ref[t, :, off:off + kp],
                                           (((1,), (1,)), ((), ())), preferred_element_type=f32)
                else:
                    part = jnp.dot(x_refs[p][...], wb_ref[t, off:off + kp, :],
                                   preferred_element_type=f32)
                acc = part if acc is None else acc + part
                off += kp
            accs.append(acc)
        epilogue(accs, e_refs, o_ref)

    @pl.when(jnp.logical_not(active))
    def _():
        o_ref[...] = jnp.zeros(o_ref.shape, o_ref.dtype)


def _gmm(xs, ws, *, bm, bn, n_cols, k_block=None, k_index=0, epilogue, extras=(), extra_specs=(),
         out_dtype, tile_group=None, num_tiles=None, slab_out=False, w_t=False, name):
    m = xs[0].shape[0]
    k_parts = tuple(x.shape[1] for x in xs) if k_block is None else (k_block,)
    k = sum(k_parts)
    n_tiles = m // bm
    if tile_group is None:
        tile_group = jnp.zeros((n_tiles,), jnp.int32)
        num_tiles = jnp.full((1,), n_tiles, jnp.int32)
    xk = k_index if k_block is not None else 0
    tid = jnp.arange(n_tiles, dtype=jnp.int32)
    first = jnp.logical_and(tid < num_tiles[0],
                            jnp.concatenate([jnp.ones((1,), bool), tile_group[1:] != tile_group[:-1]]))
    ordinal = jnp.cumsum(first.astype(jnp.int32)) - 1
    first_at = jnp.where(first, tid, n_tiles)
    nxt = jnp.concatenate([lax.cummin(first_at[::-1])[::-1][1:], jnp.full((1,), n_tiles, jnp.int32)])
    nxt = jnp.where(nxt >= n_tiles, -1, nxt).astype(jnp.int32)
    tables = jnp.concatenate([tile_group.astype(jnp.int32), nxt, ordinal.astype(jnp.int32)])
    counts = jnp.stack([num_tiles[0], ordinal[jnp.maximum(num_tiles[0] - 1, 0)] + 1]).astype(jnp.int32)

    def row(i, nt):
        return jnp.minimum(i, nt[0] - 1)

    in_specs = [pl.BlockSpec((bm, kp), lambda j, i, te, nt: (row(i, nt), xk)) for kp in k_parts]
    in_specs += [pl.BlockSpec(memory_space=pl.ANY) for _ in ws]
    in_specs += list(extra_specs)
    w_tile = (bn, k) if w_t else (k, bn)
    kern = functools.partial(_gmm_kernel, n_x=len(xs), n_w=len(ws), n_extra=len(extras),
                             k_parts=k_parts, epilogue=epilogue, w_t=w_t, n_tiles=n_tiles, bn=bn,
                             k_row0=xk * k)
    if slab_out:
        out_spec = pl.BlockSpec((bm, bn // LANES, LANES), lambda j, i, te, nt: (i, j, 0))
        out_shape = jax.ShapeDtypeStruct((m, n_cols // LANES, LANES), out_dtype)
    else:
        out_spec = pl.BlockSpec((bm, bn), lambda j, i, te, nt: (i, j))
        out_shape = jax.ShapeDtypeStruct((m, n_cols), out_dtype)
    return pl.pallas_call(
        kern,
        grid_spec=pltpu.PrefetchScalarGridSpec(
            num_scalar_prefetch=2,
            grid=(n_cols // bn, n_tiles),
            in_specs=in_specs,
            out_specs=out_spec,
            scratch_shapes=[pltpu.VMEM((len(ws),) + w_tile, f32),
                            pltpu.VMEM((len(ws),) + w_tile, bf16),
                            pltpu.SemaphoreType.DMA((1,))],
        ),
        out_shape=out_shape,
        compiler_params=_cparams(2),
        name=name,
    )(tables, counts, *xs, *ws, *extras)


def _ep_plain(accs, e_refs, o_ref):
    o_ref[...] = accs[0].astype(o_ref.dtype)


def _ep_swiglu(accs, e_refs, o_ref):
    g, u = accs
    o_ref[...] = (_silu(g) * u).astype(o_ref.dtype)


def _ep_resid(accs, e_refs, o_ref):
    res_ref, gate_ref = e_refs
    o_ref[...] = res_ref[...] + gate_ref[...] * accs[0]


def _ep_resid_partial(accs, e_refs, o_ref):
    res_ref, gate_ref, part_ref = e_refs
    o_ref[...] = res_ref[...] + gate_ref[...] * (part_ref[...] + accs[0])


def _ep_slabs(accs, e_refs, o_ref):
    _store_slabs(o_ref, accs[0])


def _ep_rope(accs, e_refs, o_ref, *, q_tiles, k_tiles, q_scale):
    cos_ref, sin_ref = e_refs
    acc = accs[0]
    j = pl.program_id(0)
    is_q = jnp.logical_and(j >= q_tiles[0], j < q_tiles[1])
    is_k = jnp.logical_and(j >= k_tiles[0], j < k_tiles[1])

    def store_roped(scale):
        rc = min(256, acc.shape[0])
        for r0 in range(0, acc.shape[0], rc):
            cos = cos_ref[r0:r0 + rc, :]
            sin = sin_ref[r0:r0 + rc, :]
            for hh in range(acc.shape[1] // LANES):
                a = acc[r0:r0 + rc, hh * LANES:(hh + 1) * LANES]
                rot = pltpu.roll(a, LANES // 2, 1)
                o_ref[r0:r0 + rc, hh * LANES:(hh + 1) * LANES] = (
                    (a * cos + rot * sin) * scale).astype(o_ref.dtype)

    @pl.when(is_q)
    def _():
        store_roped(q_scale)

    @pl.when(is_k)
    def _():
        store_roped(1.0)

    @pl.when(jnp.logical_not(jnp.logical_or(is_q, is_k)))
    def _():
        o_ref[...] = acc.astype(o_ref.dtype)


def _gate_spec(bm, bn, seq, i_gate):
    return pl.BlockSpec((None, None, 1, bn),
                        lambda j, i, te, nt: (jnp.minimum(i, nt[0] - 1) * bm // seq, i_gate, 0, j))


def _tile_spec(bm, bn):
    return pl.BlockSpec((bm, bn), lambda j, i, te, nt: (jnp.minimum(i, nt[0] - 1), j))


def _conv_kernel(val_ref, gate_ref, hval_ref, hgate_ref, w_ref, b_ref, lg_ref, lb_ref,
                 o_ref, u_scr, v_scr, *, ts, seq, rows):
    i = pl.program_id(0)
    first = (i * ts) % seq == 0
    hu = hval_ref[...].astype(f32) * _sigmoid(hgate_ref[...].astype(f32))
    u_scr[0:CONV_HALO, :] = jnp.where(first, 0.0, hu)
    u_scr[CONV_HALO:, :] = val_ref[...].astype(f32) * _sigmoid(gate_ref[...].astype(f32))
    lead = CONV_HALO - (CONV_WIDTH - 1)

    cw = LANES
    sub = 8
    wrows = rows + CONV_HALO

    def chunk(c, carry):
        r0 = pl.multiple_of(c * rows, rows)
        for c0 in range(0, u_scr.shape[1], cw):
            win = u_scr[pl.ds(r0, wrows), c0:c0 + cw]
            acc = jnp.zeros((rows, cw), f32)
            for s in range(sub):
                taps = [j for j in range(CONV_WIDTH) if (lead + j) % sub == s]
                if not taps:
                    continue
                shifted = win if s == 0 else pltpu.roll(win, wrows - s, 0)
                for j in taps:
                    a = (lead + j - s)
                    acc = acc + w_ref[j:j + 1, c0:c0 + cw] * shifted[a:a + rows, :]
            v_scr[pl.ds(r0, rows), c0:c0 + cw] = acc
        return carry

    lax.fori_loop(0, ts // rows, chunk, 0)
    u = v_scr[...] + b_ref[...]
    mu = jnp.mean(u, axis=-1, keepdims=True)
    var = jnp.mean(jnp.square(u - mu), axis=-1, keepdims=True)
    y = (u - mu) * lax.rsqrt(var + NORM_EPS) * lg_ref[...] + lb_ref[...]
    o_ref[...] = _silu(y).astype(o_ref.dtype)


def _conv(proj, conv_w, conv_b, ln_g, ln_b, seq):
    n = proj.shape[0]
    ch = conv_w.shape[1]
    ts = min(256, seq)
    hb = ts // CONV_HALO
    halo_row = lambda i: jnp.maximum(i * hb - 1, 0)
    vec = lambda: pl.BlockSpec((1, ch), lambda i: (0, 0))
    kern = functools.partial(_conv_kernel, ts=ts, seq=seq, rows=64)
    return pl.pallas_call(
        kern,
        grid=(n // ts,),
        in_specs=[
            pl.BlockSpec((ts, ch), lambda i: (i, 0)),
            pl.BlockSpec((ts, ch), lambda i: (i, 1)),
            pl.BlockSpec((CONV_HALO, ch), lambda i: (halo_row(i), 0)),
            pl.BlockSpec((CONV_HALO, ch), lambda i: (halo_row(i), 1)),
            pl.BlockSpec((CONV_WIDTH, ch), lambda i: (0, 0)),
            vec(), vec(), vec(),
        ],
        out_specs=pl.BlockSpec((ts, ch), lambda i: (i, 0)),
        out_shape=jax.ShapeDtypeStruct((n, ch), bf16),
        scratch_shapes=[pltpu.VMEM((ts + CONV_HALO, ch), f32), pltpu.VMEM((ts, ch), f32)],
        compiler_params=_cparams(1),
        name="conformer_conv",
    )(proj, proj, proj, proj, conv_w, conv_b.reshape(1, ch), ln_g.reshape(1, ch), ln_b.reshape(1, ch))


def _attn_kernel(qt_ref, kt_ref, q_ref, k_ref, v_ref, lam_ref, g_ref, o_ref, m_scr, l_scr, acc_scr,
                 *, lambda_init, hd):
    t = pl.program_id(2)
    qi = qt_ref[t]
    ki = kt_ref[t]
    tq = q_ref.shape[0]
    tk = k_ref.shape[0]
    last = ki == (qi + 1) * (tq // tk) - 1
    crosses = (ki + 1) * tk - 1 > qi * tq

    @pl.when(ki == 0)
    def _():
        m_scr[...] = jnp.full(m_scr.shape, -jnp.inf, f32)
        l_scr[...] = jnp.zeros(l_scr.shape, f32)
        acc_scr[...] = jnp.zeros(acc_scr.shape, f32)

    def lane_fold(x, op):
        out = x[:, 0:LANES]
        for cb in range(1, x.shape[1] // LANES):
            out = op(out, x[:, cb * LANES:(cb + 1) * LANES])
        return out

    def step(masked):
        v = v_ref[...]
        for s in range(2):
            q = q_ref[:, s * hd:(s + 1) * hd]
            k = k_ref[:, s * hd:(s + 1) * hd]
            sc = lax.dot_general(q, k, (((1,), (1,)), ((), ())), preferred_element_type=f32)
            if masked:
                row = lax.broadcasted_iota(jnp.int32, sc.shape, 0)
                col = lax.broadcasted_iota(jnp.int32, sc.shape, 1)
                sc = jnp.where(col + (ki * tk - qi * tq) <= row, sc, -jnp.inf)
            m_cur = jnp.max(lane_fold(sc, jnp.maximum), axis=-1, keepdims=True)
            m_prev = m_scr[s]
            m_new = jnp.maximum(m_prev, m_cur)
            alpha = jnp.exp2(m_prev - m_new)
            p = jnp.exp2(sc - jnp.tile(m_new, (1, sc.shape[1] // LANES)))
            l_scr[s] = alpha * l_scr[s] + lane_fold(p, jnp.add)
            acc_scr[s] = (jnp.tile(alpha, (1, v.shape[1] // LANES)) * acc_scr[s]
                          + jnp.dot(p.astype(bf16), v, preferred_element_type=f32))
            m_scr[s] = m_new

    @pl.when(jnp.logical_not(crosses))
    def _():
        step(False)

    @pl.when(crosses)
    def _():
        step(True)

    @pl.when(last)
    def _():
        lp = lam_ref[...]
        lam = (jnp.exp(jnp.sum(lp[0:1] * lp[1:2], axis=-1, keepdims=True))
               - jnp.exp(jnp.sum(lp[2:3] * lp[3:4], axis=-1, keepdims=True)) + lambda_init)
        l0 = jnp.sum(l_scr[0], axis=-1, keepdims=True)
        l1 = jnp.sum(l_scr[1], axis=-1, keepdims=True)
        o = acc_scr[0] / l0 - lam * (acc_scr[1] / l1)
        o = o * lax.rsqrt(jnp.mean(o * o, axis=-1, keepdims=True) + DIFF_SUBLN_EPS)
        o_ref[...] = ((o * g_ref[...]) * (1.0 - lambda_init)).astype(o_ref.dtype)


def _diff_attention(proj, lam_p, subln_g, bsz, seq, q_col0, k_col0, v_col0, lambda_init):
    n = proj.shape[0]
    hd = lam_p.shape[1]
    vd = 2 * hd
    tq = min(1024, seq)
    tk = min(512, seq)
    nq = seq // tq
    nk = seq // tk
    pairs = [(qi, ki) for qi in range(nq) for ki in range((qi + 1) * (tq // tk))]
    q_of = jnp.asarray([p[0] for p in pairs], jnp.int32)
    k_of = jnp.asarray([p[1] for p in pairs], jnp.int32)
    kern = functools.partial(_attn_kernel, lambda_init=lambda_init, hd=hd)
    return pl.pallas_call(
        kern,
        grid_spec=pltpu.PrefetchScalarGridSpec(
            num_scalar_prefetch=2,
            grid=(bsz, DIFF_HEADS, len(pairs)),
            in_specs=[
                pl.BlockSpec((tq, vd), lambda b, h, t, qt, kt: (b * nq + qt[t], q_col0 // vd + h)),
                pl.BlockSpec((tk, vd), lambda b, h, t, qt, kt: (b * nk + kt[t], k_col0 // vd + h)),
                pl.BlockSpec((tk, vd), lambda b, h, t, qt, kt: (b * nk + kt[t], v_col0 // vd + h)),
                pl.BlockSpec((4, hd), lambda b, h, t, qt, kt: (0, 0)),
                pl.BlockSpec((1, vd), lambda b, h, t, qt, kt: (0, 0)),
            ],
            out_specs=pl.BlockSpec((tq, vd), lambda b, h, t, qt, kt: (b * nq + qt[t], h)),
            scratch_shapes=[pltpu.VMEM((2, tq, LANES), f32), pltpu.VMEM((2, tq, LANES), f32),
                            pltpu.VMEM((2, tq, vd), f32)],
        ),
        out_shape=jax.ShapeDtypeStruct((n, DIFF_HEADS * vd), bf16),
        compiler_params=_cparams(3),
        name="diff_attention",
    )(q_of, k_of, proj, proj, proj, lam_p, subln_g.reshape(1, vd))


def _loga_kernel(h_ref, w1_ref, w2_ref, b_ref, o_ref):
    rowi = lax.broadcasted_iota(jnp.int32, w1_ref.shape, 0)
    w1 = jnp.where(rowi < GLA_RANK, w1_ref[...], 0.0).astype(bf16)
    g1 = lax.dot_general(h_ref[...], w1, (((1,), (1,)), ((), ())), preferred_element_type=f32)
    gpre = jnp.dot(g1.astype(bf16), w2_ref[...].astype(bf16), preferred_element_type=f32) + b_ref[...]
    nx = -gpre
    softplus = jnp.maximum(nx, 0.0) + jnp.log1p(jnp.exp(-jnp.abs(nx)))
    o_ref[...] = -softplus / GLA_TAU


def _log_decay(h, w_in_t, col0, w2, bias):
    n, d = h.shape
    kdim = w2.shape[1]
    bm = min(512, n)
    w2p = jnp.zeros((LANES, kdim), f32).at[:GLA_RANK].set(w2)
    return pl.pallas_call(
        _loga_kernel,
        grid=(n // bm,),
        in_specs=[
            pl.BlockSpec((bm, d), lambda i: (i, 0)),
            pl.BlockSpec((None, LANES, d), lambda i: (0, col0 // LANES, 0)),
            pl.BlockSpec((LANES, kdim), lambda i: (0, 0)),
            pl.BlockSpec((1, kdim), lambda i: (0, 0)),
        ],
        out_specs=pl.BlockSpec((bm, kdim), lambda i: (i, 0)),
        out_shape=jax.ShapeDtypeStruct((n, kdim), f32),
        compiler_params=_cparams(1),
        name="gla_log_decay",
    )(h, w_in_t, w2p, bias.reshape(1, kdim))


def _gla_kernel(q_ref, k_ref, v_ref, la_ref, r_ref, g_ref, o_ref, st_ref, *, scale):
    c = pl.program_id(2)

    @pl.when(c == 0)
    def _():
        st_ref[...] = jnp.zeros(st_ref.shape, f32)

    ch = q_ref.shape[0]
    dv = v_ref.shape[1]
    la = la_ref[...]
    row = lax.broadcasted_iota(jnp.int32, (ch, ch), 0)
    col = lax.broadcasted_iota(jnp.int32, (ch, ch), 1)
    causal = row >= col
    dk = la.shape[1]
    la_hi = la.astype(bf16)
    la_lo = (la - la_hi.astype(f32)).astype(bf16)
    la2 = jnp.concatenate([la_hi, la_lo], axis=1)
    b2 = jnp.dot(causal.astype(bf16), la2, preferred_element_type=f32)
    b = b2[:, :dk] + b2[:, dk:]
    b_last = b[ch - 1:ch, :]
    b_mid = b[ch // 2 - 1:ch // 2, :]
    s2 = lax.dot_general(la2, jnp.ones((ch, LANES), bf16), (((0,), (0,)), ((), ())),
                         preferred_element_type=f32)
    b_last_col = s2[:dk] + s2[dk:]
    q = q_ref[...].astype(f32) * scale
    k = k_ref[...].astype(f32)
    q_t = (q * jnp.exp(b)).astype(bf16)
    q_rel = (q * jnp.exp(b - b_mid)).astype(bf16)
    k_rel = (k * jnp.exp(b_mid - b)).astype(bf16)
    k_dec = (k * jnp.exp(b_last - b)).astype(bf16)
    v = v_ref[...]
    attn = lax.dot_general(q_rel, k_rel, (((1,), (1,)), ((), ())), preferred_element_type=f32)
    attn = jnp.where(causal, attn, 0.0)
    st = st_ref[...]
    o = (jnp.dot(attn.astype(bf16), v, preferred_element_type=f32)
         + jnp.dot(q_t, st.astype(bf16), preferred_element_type=f32))
    kv = lax.dot_general(k_dec, v, (((0,), (0,)), ((), ())), preferred_element_type=f32)
    st_ref[...] = jnp.tile(jnp.exp(b_last_col), (1, dv // LANES)) * st + kv
    o = o * lax.rsqrt(jnp.mean(o * o, axis=-1, keepdims=True) + NORM_EPS) * g_ref[...]
    o_ref[...] = (o * _silu(r_ref[...].astype(f32))).astype(o_ref.dtype)


def _gla(proj, log_a, norm_g, bsz, seq, dk, dv):
    n = proj.shape[0]
    ch = 2 * GLA_CHUNK if seq % (2 * GLA_CHUNK) == 0 else GLA_CHUNK
    nc = seq // ch
    kcols = GLA_HEADS * dk
    vcols = GLA_HEADS * dv
    rowb = lambda b, h, c: b * nc + c
    kern = functools.partial(_gla_kernel, scale=dk ** -0.5)
    return pl.pallas_call(
        kern,
        grid=(bsz, GLA_HEADS, nc),
        in_specs=[
            pl.BlockSpec((ch, dk), lambda b, h, c: (rowb(b, h, c), h)),
            pl.BlockSpec((ch, dk), lambda b, h, c: (rowb(b, h, c), kcols // dk + h)),
            pl.BlockSpec((ch, dv), lambda b, h, c: (rowb(b, h, c), 2 * kcols // dv + h)),
            pl.BlockSpec((ch, dk), lambda b, h, c: (rowb(b, h, c), h)),
            pl.BlockSpec((ch, dv), lambda b, h, c: (rowb(b, h, c), (2 * kcols + vcols) // dv + h)),
            pl.BlockSpec((1, dv), lambda b, h, c: (0, 0)),
        ],
        out_specs=pl.BlockSpec((ch, dv), lambda b, h, c: (rowb(b, h, c), h)),
        out_shape=jax.ShapeDtypeStruct((n, vcols), bf16),
        scratch_shapes=[pltpu.VMEM((dk, dv), f32)],
        compiler_params=_cparams(3),
        name="gla_chunked",
    )(proj, proj, proj, log_a, proj, norm_g.reshape(1, dv))


GATHER_UNROLL = 8


def _slab_copy(src_hbm, dst_vmem, sem, src_row, dst_row):
    return pltpu.make_async_copy(src_hbm.at[pl.ds(src_row, 1)], dst_vmem.at[pl.ds(dst_row, 1)], sem)


def _issue_slabs(idx_ref, n_idx, src_hbm, dst, sem):
    def body(r2, carry):
        for u in range(2):
            r = 2 * r2 + u
            _slab_copy(src_hbm, dst, sem, idx_ref[0, r], r).start(priority=u)
        return carry

    lax.fori_loop(0, n_idx // 2, body, 0, unroll=GATHER_UNROLL // 2)


def _wait_slabs(n_idx, src_hbm, dst, sem):
    def body(r, carry):
        _slab_copy(src_hbm, dst, sem, 0, r).wait()
        return carry

    lax.fori_loop(0, n_idx, body, 0, unroll=GATHER_UNROLL)


def _fetch_slabs(cur_ref, nxt_ref, src_hbm, buf, sem, n_idx):
    s = pl.program_id(0)
    slot = s % 2

    @pl.when(s == 0)
    def _():
        _issue_slabs(cur_ref, n_idx, src_hbm, buf.at[0], sem.at[0])

    @pl.when(s + 1 < pl.num_programs(0))
    def _():
        _issue_slabs(nxt_ref, n_idx, src_hbm, buf.at[1 - slot], sem.at[1 - slot])

    _wait_slabs(n_idx, src_hbm, buf.at[slot], sem.at[slot])
    return slot


def _idx_specs(n_steps, n_idx):
    return [
        pl.BlockSpec((None, 1, n_idx), lambda i: (i, 0, 0), memory_space=pltpu.SMEM),
        pl.BlockSpec((None, 1, n_idx), lambda i: (jnp.minimum(i + 1, n_steps - 1), 0, 0),
                     memory_space=pltpu.SMEM),
    ]


def _gather_kernel(cur_ref, nxt_ref, h_hbm, o_ref, buf, sem, *, tr):
    slot = _fetch_slabs(cur_ref, nxt_ref, h_hbm, buf, sem, tr)
    o_ref[...] = _unpack_bf16_pairs(buf[slot].reshape(tr, o_ref.shape[1] // 2))


def _gather_rows(h, row_token, tr):
    n, nslab, _ = h.shape
    r = row_token.shape[0]
    kern = functools.partial(_gather_kernel, tr=tr)
    idx = row_token.reshape(r // tr, 1, tr)
    return pl.pallas_call(
        kern,
        grid=(r // tr,),
        in_specs=_idx_specs(r // tr, tr) + [pl.BlockSpec(memory_space=pl.ANY)],
        out_specs=pl.BlockSpec((tr, 2 * nslab * LANES), lambda i: (i, 0)),
        out_shape=jax.ShapeDtypeStruct((r, 2 * nslab * LANES), bf16),
        scratch_shapes=[pltpu.VMEM((2, tr, nslab, LANES), jnp.uint32), pltpu.SemaphoreType.DMA((2,))],
        compiler_params=_cparams(1),
        name="moe_gather",
    )(idx, idx, h)


def _combine_kernel(cur_ref, nxt_ref, y_hbm, w_ref, x_ref, gate_ref, fn_ref, o_ref, buf, sem, *, tt):
    slot = _fetch_slabs(cur_ref, nxt_ref, y_hbm, buf, sem, TOP_K * tt)
    ys = buf[slot].reshape(TOP_K * tt, x_ref.shape[1])
    w = w_ref[...]
    moe = w[:, 0:1] * ys[0:tt] + w[:, 1:2] * ys[tt:2 * tt]
    x = x_ref[...] + gate_ref[...] * moe
    y = x * lax.rsqrt(jnp.mean(x * x, axis=-1, keepdims=True) + NORM_EPS)
    o_ref[...] = y * fn_ref[...]


def _combine(y, pos, top_w, x, mod, i_gate, final_norm, seq):
    n, d = x.shape
    nslab = d // LANES
    tt = min(64, seq)
    kern = functools.partial(_combine_kernel, tt=tt)
    idx = pos.reshape(n // tt, tt, TOP_K).transpose(0, 2, 1).reshape(n // tt, 1, TOP_K * tt)
    return pl.pallas_call(
        kern,
        grid=(n // tt,),
        in_specs=_idx_specs(n // tt, TOP_K * tt) + [
            pl.BlockSpec(memory_space=pl.ANY),
            pl.BlockSpec((tt, LANES), lambda i: (i, 0)),
            pl.BlockSpec((tt, d), lambda i: (i, 0)),
            pl.BlockSpec((None, None, 1, d), lambda i: (i * tt // seq, i_gate, 0, 0)),
            pl.BlockSpec((1, d), lambda i: (0, 0)),
        ],
        out_specs=pl.BlockSpec((tt, d), lambda i: (i, 0)),
        out_shape=jax.ShapeDtypeStruct((n, d), f32),
        scratch_shapes=[pltpu.VMEM((2, TOP_K * tt, nslab, LANES), f32),
                        pltpu.SemaphoreType.DMA((2,))],
        compiler_params=_cparams(1),
        name="moe_combine_final_norm",
    )(idx, idx, y, top_w, x, mod, final_norm.reshape(1, d))


def _routing_tables(top_i, bm):
    n = top_i.shape[0]
    e_flat = top_i.reshape(-1)
    onehot = (e_flat[:, None] == jnp.arange(N_EXPERTS)[None, :]).astype(jnp.int32)
    counts = jnp.sum(onehot, axis=0)
    rank = jnp.take_along_axis(jnp.cumsum(onehot, axis=0) - onehot, e_flat[:, None], axis=1)[:, 0]
    padded = ((counts + bm - 1) // bm) * bm
    ends = jnp.cumsum(padded)
    starts = ends - padded
    pos = starts[e_flat] + rank
    n_rows = n * TOP_K + N_EXPERTS * bm
    n_tiles = n_rows // bm
    num_tiles = (ends[-1] // bm).astype(jnp.int32)
    tile_start = jnp.minimum(jnp.arange(n_tiles), num_tiles - 1) * bm
    tile_group = jnp.minimum(jnp.searchsorted(ends, tile_start, side="right"),
                             N_EXPERTS - 1).astype(jnp.int32)
    row_token = jnp.zeros((n_rows,), jnp.int32).at[pos].set(jnp.arange(n * TOP_K, dtype=jnp.int32) // TOP_K)
    return pos.astype(jnp.int32), row_token, tile_group, num_tiles.reshape(1)


def _rope_tables(seq, hd):
    inv = ROPE_THETA ** (-jnp.arange(0, hd, 2, dtype=f32) / hd)
    ang = jnp.arange(seq).astype(f32)[:, None] * inv[None, :]
    cos, sin = jnp.cos(ang), jnp.sin(ang)
    return jnp.concatenate([cos, cos], axis=-1), jnp.concatenate([-sin, sin], axis=-1)


def kernel(x, c, norm_gains, ada_w, ada_b, e_w_in, e_conv_w, e_conv_b, e_conv_ln_g, e_conv_ln_b, e_diff_lambda, e_diff_subln, e_w_out, e_ffn_gate, e_ffn_up, e_ffn_down, o_w_in, o_gate_w2, o_gate_b, o_gla_norm, o_w_out, o_router, o_exp_gate, o_exp_up, o_exp_down, final_norm):
    bsz, seq, d = x.shape
    n = bsz * seq
    depth = ada_w.shape[0]
    assert depth == 2, "trunk is one even (conv + diff-attn) and one odd (GLA + experts) layer"
    xf = x.reshape(n, d)
    mods = _ada(c, ada_w, ada_b)
    bm = min(1024, seq)

    mod = mods[0]
    conv_ch = e_conv_w.shape[2]
    hd = e_diff_lambda.shape[2]
    qk_cols = 2 * DIFF_HEADS * hd
    q_col0 = 2 * conv_ch
    k_col0 = q_col0 + qk_cols
    v_col0 = k_col0 + qk_cols
    even_in = e_w_in.shape[2]
    bn = 512
    h = _normmod(xf, norm_gains[0, 0], mod, 0, 1, seq)
    cos_t, sin_t = _rope_tables(seq, hd)
    bmi = min(1024, seq)
    bni = max(t for t in (1024, 512, 256)
              if all(cc % t == 0 for cc in (q_col0, k_col0, v_col0, even_in)))
    rope_spec = pl.BlockSpec(
        (bmi, hd), lambda j, i, te, nt: (jnp.minimum(i, nt[0] - 1) % (seq // bmi), 0))
    ep = functools.partial(_ep_rope, q_tiles=(q_col0 // bni, k_col0 // bni),
                           k_tiles=(k_col0 // bni, v_col0 // bni),
                           q_scale=hd ** -0.5 * math.log2(math.e))
    proj = _gmm([h], [e_w_in], bm=bmi, bn=bni, n_cols=even_in, epilogue=ep,
                extras=(cos_t, sin_t), extra_specs=(rope_spec, rope_spec),
                out_dtype=bf16, name="even_in_proj")
    y_a = _conv(proj, e_conv_w[0], e_conv_b[0], e_conv_ln_g[0], e_conv_ln_b[0], seq)
    lambda_init = 0.8 - 0.6 * math.exp(-0.3 * 0)
    y_b = _diff_attention(proj, e_diff_lambda[0], e_diff_subln[0], bsz, seq,
                          q_col0, k_col0, v_col0, lambda_init)
    xf = _gmm([y_a, y_b], [e_w_out], bm=bm, bn=bn, n_cols=d, epilogue=_ep_resid,
              extras=(xf, mod), extra_specs=(_tile_spec(bm, bn), _gate_spec(bm, bn, seq, 2)),
              out_dtype=f32, name="even_out_proj")
    h = _normmod(xf, norm_gains[0, 1], mod, 3, 4, seq)
    d_ff = e_ffn_gate.shape[2]
    bnf = 256
    act = _gmm([h], [e_ffn_gate, e_ffn_up], bm=min(2048, seq), bn=bnf, n_cols=d_ff,
               epilogue=_ep_swiglu, out_dtype=bf16, name="ffn_gate_up")
    half = d_ff // 2
    bmd = min(1024, seq)
    part = _gmm([act], [e_ffn_down], bm=bmd, bn=bn, n_cols=d, k_block=half, k_index=0,
                epilogue=_ep_plain, out_dtype=f32, name="ffn_down_lo")
    xf = _gmm([act], [e_ffn_down], bm=bmd, bn=bn, n_cols=d, k_block=half, k_index=1,
              epilogue=_ep_resid_partial,
              extras=(xf, mod, part),
              extra_specs=(_tile_spec(bmd, bn), _gate_spec(bmd, bn, seq, 5), _tile_spec(bmd, bn)),
              out_dtype=f32, name="ffn_down_hi")

    mod = mods[1]
    dk = o_gate_w2.shape[2] // GLA_HEADS
    dv = o_gla_norm.shape[1]
    kcols = GLA_HEADS * dk
    vcols = GLA_HEADS * dv
    main_cols = 2 * kcols + 2 * vcols
    h = _normmod(xf, norm_gains[1, 0], mod, 0, 1, seq)
    w_in_t = jnp.swapaxes(o_w_in, 1, 2)
    proj = _gmm([h], [w_in_t], bm=bm, bn=1024, n_cols=main_cols, epilogue=_ep_plain,
                out_dtype=bf16, w_t=True, name="odd_in_proj")
    log_a = _log_decay(h, w_in_t, main_cols, o_gate_w2[0], o_gate_b[0])
    o = _gla(proj, log_a, o_gla_norm[0], bsz, seq, dk, dv)
    xf = _gmm([o], [o_w_out], bm=bm, bn=bn, n_cols=d, epilogue=_ep_resid,
              extras=(xf, mod), extra_specs=(_tile_spec(bm, bn), _gate_spec(bm, bn, seq, 2)),
              out_dtype=f32, name="odd_out_proj")
    h32, top_i, top_w = _normmod_router(xf, norm_gains[1, 1], mod, 3, 4, seq, o_router[0])
    bme = min(512, seq)
    pos, row_token, tile_group, num_tiles = _routing_tables(top_i, bme)
    hs = _gather_rows(h32, row_token, min(128, seq))
    d_fe = o_exp_gate.shape[3]
    act = _gmm([hs], [o_exp_gate[0], o_exp_up[0]], bm=bme, bn=512, n_cols=d_fe,
               epilogue=_ep_swiglu, out_dtype=bf16, tile_group=tile_group, num_tiles=num_tiles,
               name="expert_gate_up")
    ye = _gmm([act], [o_exp_down[0]], bm=bme, bn=1024, n_cols=d, epilogue=_ep_slabs,
              out_dtype=f32, tile_group=tile_group, num_tiles=num_tiles, slab_out=True,
              name="expert_down")
    out = _combine(ye, pos, top_w, xf, mod, 5, final_norm, seq)
    return out.reshape(bsz, seq, d)
```

```python
import functools
import math

import jax
import jax.numpy as jnp
from jax import lax
from jax.experimental import pallas as pl
from jax.experimental.pallas import tpu as pltpu

f32 = jnp.float32
bf16 = jnp.bfloat16

N_MOD = 6
NORM_EPS = 1e-6
CONV_WIDTH = 31
CONV_HALO = 32
DIFF_HEADS = 8
DIFF_SUBLN_EPS = 1e-5
ROPE_THETA = 10000.0
GLA_HEADS = 4
GLA_RANK = 16
GLA_TAU = 16.0
GLA_CHUNK = 64
N_EXPERTS = 8
TOP_K = 2
LANES = 128
VMEM_LIMIT_BYTES = 58 * 1024 * 1024

TILE = dict(
    ada_cols=512, norm_rows=512, router_rows=256, logdecay_rows=512,
    rows=1024, cols=512,
    in_cols=1024,
    ffn_rows=2048, ffn_cols=256,
    expert_rows=512, expert_up_cols=512, expert_down_cols=1024,
    gather_rows=256, combine_rows=128,
    attn_q=1024, attn_kv=512, conv_rows=256, conv_chunk=64)


def _cparams(n_axes):
    return pltpu.CompilerParams(
        dimension_semantics=("arbitrary",) * n_axes,
        vmem_limit_bytes=VMEM_LIMIT_BYTES)


def _sigmoid(x):
    return 1.0 / (1.0 + jnp.exp(-x))


def _silu(x):
    return x * _sigmoid(x)


def _ada_kernel(c_ref, w_ref, b_ref, o_ref):
    ca = _silu(c_ref[...]).astype(bf16)
    w = w_ref[...].astype(bf16)
    o_ref[...] = jnp.dot(ca, w, preferred_element_type=f32) + b_ref[...]


def _ada(c, ada_w, ada_b):
    depth, d, n6 = ada_w.shape
    bsz = c.shape[0]
    rows = 8
    cp = jnp.zeros((rows, d), f32).at[:bsz].set(c)
    bn = min(TILE["ada_cols"], n6)
    out = pl.pallas_call(
        _ada_kernel,
        grid=(depth, n6 // bn),
        in_specs=[
            pl.BlockSpec((rows, d), lambda l, j: (0, 0)),
            pl.BlockSpec((None, d, bn), lambda l, j: (l, 0, j)),
            pl.BlockSpec((None, 1, bn), lambda l, j: (l, 0, j)),
        ],
        out_specs=pl.BlockSpec((None, rows, bn), lambda l, j: (l, 0, j)),
        out_shape=jax.ShapeDtypeStruct((depth, rows, n6), f32),
        compiler_params=_cparams(2),
        name="ada_mod",
    )(cp, ada_w, ada_b.reshape(depth, 1, n6))
    return out[:, :bsz].reshape(depth, bsz, N_MOD, 1, d)


def _normmod_value(x_ref, g_ref, sh_ref, sc_ref):
    x = x_ref[...]
    y = x * lax.rsqrt(jnp.mean(x * x, axis=-1, keepdims=True) + NORM_EPS)
    y = y * g_ref[...]
    return y * (1.0 + sc_ref[...]) + sh_ref[...]


def _normmod_kernel(x_ref, g_ref, sh_ref, sc_ref, o_ref):
    o_ref[...] = _normmod_value(x_ref, g_ref, sh_ref, sc_ref).astype(o_ref.dtype)


def _normmod_specs(bm, d, seq, i_shift, i_scale):
    return [
        pl.BlockSpec((bm, d), lambda i: (i, 0)),
        pl.BlockSpec((1, d), lambda i: (0, 0)),
        pl.BlockSpec((None, None, 1, d), lambda i: (i * bm // seq, i_shift, 0, 0)),
        pl.BlockSpec((None, None, 1, d), lambda i: (i * bm // seq, i_scale, 0, 0)),
    ]


def _normmod(x, gain, mod, i_shift, i_scale, seq):
    n, d = x.shape
    bm = min(TILE["norm_rows"], seq)
    return pl.pallas_call(
        _normmod_kernel,
        grid=(n // bm,),
        in_specs=_normmod_specs(bm, d, seq, i_shift, i_scale),
        out_specs=pl.BlockSpec((bm, d), lambda i: (i, 0)),
        out_shape=jax.ShapeDtypeStruct((n, d), bf16),
        compiler_params=_cparams(1),
        name="normmod",
    )(x, gain.reshape(1, d), mod, mod)


def _store_slabs(o_ref, val):
    o_ref[...] = val.reshape(o_ref.shape)


HI16 = 0xFFFF0000


def _pack_bf16_pairs(v):
    half = v.shape[1] // 2
    lo = pltpu.bitcast(v[:, :half].astype(bf16).astype(f32), jnp.uint32) >> 16
    hi = pltpu.bitcast(v[:, half:].astype(bf16).astype(f32), jnp.uint32) & jnp.uint32(HI16)
    return lo | hi


def _unpack_bf16_pairs(w):
    lo = pltpu.bitcast(w << 16, f32).astype(bf16)
    hi = pltpu.bitcast(w & jnp.uint32(HI16), f32).astype(bf16)
    return jnp.concatenate([lo, hi], axis=1)


def _normmod_router_kernel(x_ref, g_ref, sh_ref, sc_ref, wr_ref, h_ref, idx_ref, wgt_ref):
    h = _normmod_value(x_ref, g_ref, sh_ref, sc_ref)
    _store_slabs(h_ref, _pack_bf16_pairs(h))
    wr = wr_ref[...]
    h_hi = h.astype(bf16)
    h_lo = (h - h_hi.astype(f32)).astype(bf16)
    w_hi = wr.astype(bf16)
    w_lo = (wr - w_hi.astype(f32)).astype(bf16)
    logits = (jnp.dot(h_hi, w_hi, preferred_element_type=f32)
              + jnp.dot(h_hi, w_lo, preferred_element_type=f32)
              + jnp.dot(h_lo, w_hi, preferred_element_type=f32))
    col = lax.broadcasted_iota(jnp.int32, logits.shape, 1)
    neg = jnp.float32(-jnp.inf)
    l0 = jnp.where(col < N_EXPERTS, logits, neg)
    m1 = jnp.max(l0, axis=-1, keepdims=True)
    i1 = jnp.min(jnp.where(l0 == m1, col, LANES), axis=-1, keepdims=True)
    l1 = jnp.where(col == i1, neg, l0)
    m2 = jnp.max(l1, axis=-1, keepdims=True)
    i2 = jnp.min(jnp.where(l1 == m2, col, LANES), axis=-1, keepdims=True)
    e = jnp.exp(m2 - m1)
    w1 = 1.0 / (1.0 + e)
    w2 = e / (1.0 + e)
    idx_ref[...] = jnp.where(col == 0, i1, jnp.where(col == 1, i2, 0))
    wgt_ref[...] = jnp.where(col == 0, w1, jnp.where(col == 1, w2, 0.0))


def _normmod_router(x, gain, mod, i_shift, i_scale, seq, w_router):
    n, d = x.shape
    bm = min(TILE["router_rows"], seq)
    wr = jnp.zeros((d, LANES), f32).at[:, :N_EXPERTS].set(w_router)
    h, idx, wgt = pl.pallas_call(
        _normmod_router_kernel,
        grid=(n // bm,),
        in_specs=_normmod_specs(bm, d, seq, i_shift, i_scale)
        + [pl.BlockSpec((d, LANES), lambda i: (0, 0))],
        out_specs=[
            pl.BlockSpec((bm, d // 2 // LANES, LANES), lambda i: (i, 0, 0)),
            pl.BlockSpec((bm, LANES), lambda i: (i, 0)),
            pl.BlockSpec((bm, LANES), lambda i: (i, 0)),
        ],
        out_shape=[
            jax.ShapeDtypeStruct((n, d // 2 // LANES, LANES), jnp.uint32),
            jax.ShapeDtypeStruct((n, LANES), jnp.int32),
            jax.ShapeDtypeStruct((n, LANES), f32),
        ],
        compiler_params=_cparams(1),
        name="normmod_router",
    )(x, gain.reshape(1, d), mod, mod, wr)
    return h, idx[:, :TOP_K], wgt


def _gmm_kernel(te_ref, nt_ref, *refs, n_x, n_w, n_extra, k_parts, epilogue, w_t, n_tiles, bn, k_row0):
    x_refs = refs[:n_x]
    w_hbm = refs[n_x:n_x + n_w]
    e_refs = refs[n_x + n_w:n_x + n_w + n_extra]
    o_ref = refs[n_x + n_w + n_extra]
    wf_ref, wb_ref, sem = refs[n_x + n_w + n_extra + 1:]
    j = pl.program_id(0)
    i = pl.program_id(1)
    active = i < nt_ref[0]
    prev = te_ref[jnp.maximum(i - 1, 0)]
    changed = jnp.logical_and(active, jnp.logical_or(i == 0, te_ref[i] != prev))
    k = sum(k_parts)

    def w_copy(t, g, jj):
        c0 = pl.multiple_of(jj * bn, bn)
        if w_t:
            src = w_hbm[t].at[g, pl.ds(c0, bn), pl.ds(k_row0, k)]
        else:
            src = w_hbm[t].at[g, pl.ds(k_row0, k), pl.ds(c0, bn)]
        return pltpu.make_async_copy(src, wf_ref.at[t], sem.at[0])

    @pl.when(changed)
    def _():
        run = j * nt_ref[1] + te_ref[2 * n_tiles + i]
        g = te_ref[i]

        @pl.when(run == 0)
        def _():
            for t in range(n_w):
                w_copy(t, g, j).start()

        for t in range(n_w):
            w_copy(t, g, j).wait()
        for t in range(n_w):
            wb_ref[t] = wf_ref[t].astype(bf16)
        nxt = te_ref[n_tiles + i]
        more_groups = nxt >= 0
        g2 = jnp.where(more_groups, te_ref[jnp.maximum(nxt, 0)], te_ref[0])
        j2 = jnp.where(more_groups, j, j + 1)

        @pl.when(j2 < pl.num_programs(0))
        def _():
            for t in range(n_w):
                w_copy(t, g2, j2).start()

    @pl.when(active)
    def _():
        accs = []
        for t in range(n_w):
            acc = None
            off = 0
            for p in range(n_x):
                kp = k_parts[p]
                if w_t:
                    part = lax.dot_general(x_refs[p][...], wb_ref[t, :, off:off + kp],
                                           (((1,), (1,)), ((), ())), preferred_element_type=f32)
                else:
                    part = jnp.dot(x_refs[p][...], wb_ref[t, off:off + kp, :],
                                   preferred_element_type=f32)
                acc = part if acc is None else acc + part
                off += kp
            accs.append(acc)
        epilogue(accs, e_refs, o_ref)

    @pl.when(jnp.logical_not(active))
    def _():
        o_ref[...] = jnp.zeros(o_ref.shape, o_ref.dtype)


def _gmm(xs, ws, *, bm, bn, n_cols, k_block=None, k_index=0, epilogue, extras=(), extra_specs=(),
         out_dtype, tile_group=None, num_tiles=None, slab_out=False, w_t=False, name):
    m = xs[0].shape[0]
    k_parts = tuple(x.shape[1] for x in xs) if k_block is None else (k_block,)
    k = sum(k_parts)
    n_tiles = m // bm
    if tile_group is None:
        tile_group = jnp.zeros((n_tiles,), jnp.int32)
        num_tiles = jnp.full((1,), n_tiles, jnp.int32)
    xk = k_index if k_block is not None else 0
    tid = jnp.arange(n_tiles, dtype=jnp.int32)
    first = jnp.logical_and(tid < num_tiles[0],
                            jnp.concatenate([jnp.ones((1,), bool), tile_group[1:] != tile_group[:-1]]))
    ordinal = jnp.cumsum(first.astype(jnp.int32)) - 1
    first_at = jnp.where(first, tid, n_tiles)
    nxt = jnp.concatenate([lax.cummin(first_at[::-1])[::-1][1:], jnp.full((1,), n_tiles, jnp.int32)])
    nxt = jnp.where(nxt >= n_tiles, -1, nxt).astype(jnp.int32)
    tables = jnp.concatenate([tile_group.astype(jnp.int32), nxt, ordinal.astype(jnp.int32)])
    counts = jnp.stack([num_tiles[0], ordinal[jnp.maximum(num_tiles[0] - 1, 0)] + 1]).astype(jnp.int32)

    def row(i, nt):
        return jnp.minimum(i, nt[0] - 1)

    in_specs = [pl.BlockSpec((bm, kp), lambda j, i, te, nt: (row(i, nt), xk)) for kp in k_parts]
    in_specs += [pl.BlockSpec(memory_space=pl.ANY) for _ in ws]
    in_specs += list(extra_specs)
    w_tile = (bn, k) if w_t else (k, bn)
    kern = functools.partial(_gmm_kernel, n_x=len(xs), n_w=len(ws), n_extra=len(extras),
                             k_parts=k_parts, epilogue=epilogue, w_t=w_t, n_tiles=n_tiles, bn=bn,
                             k_row0=xk * k)
    if slab_out:
        out_spec = pl.BlockSpec((bm, bn // LANES, LANES), lambda j, i, te, nt: (i, j, 0))
        out_shape = jax.ShapeDtypeStruct((m, n_cols // LANES, LANES), out_dtype)
    else:
        out_spec = pl.BlockSpec((bm, bn), lambda j, i, te, nt: (i, j))
        out_shape = jax.ShapeDtypeStruct((m, n_cols), out_dtype)
    return pl.pallas_call(
        kern,
        grid_spec=pltpu.PrefetchScalarGridSpec(
            num_scalar_prefetch=2,
            grid=(n_cols // bn, n_tiles),
            in_specs=in_specs,
            out_specs=out_spec,
            scratch_shapes=[pltpu.VMEM((len(ws),) + w_tile, f32),
                            pltpu.VMEM((len(ws),) + w_tile, bf16),
                            pltpu.SemaphoreType.DMA((1,))],
        ),
        out_shape=out_shape,
        compiler_params=_cparams(2),
        name=name,
    )(tables, counts, *xs, *ws, *extras)


def _ep_plain(accs, e_refs, o_ref):
    o_ref[...] = accs[0].astype(o_ref.dtype)


def _ep_swiglu(accs, e_refs, o_ref):
    g, u = accs
    o_ref[...] = (_silu(g) * u).astype(o_ref.dtype)


def _ep_resid(accs, e_refs, o_ref):
    res_ref, gate_ref = e_refs
    o_ref[...] = res_ref[...] + gate_ref[...] * accs[0]


def _ep_resid_partial(accs, e_refs, o_ref):
    res_ref, gate_ref, part_ref = e_refs
    o_ref[...] = res_ref[...] + gate_ref[...] * (part_ref[...] + accs[0])


def _ep_slabs(accs, e_refs, o_ref):
    _store_slabs(o_ref, accs[0])


def _ep_rope(accs, e_refs, o_ref, *, q_tiles, k_tiles, q_scale):
    cos_ref, sin_ref = e_refs
    acc = accs[0]
    j = pl.program_id(0)
    is_q = jnp.logical_and(j >= q_tiles[0], j < q_tiles[1])
    is_k = jnp.logical_and(j >= k_tiles[0], j < k_tiles[1])

    def store_roped(scale):
        rc = min(256, acc.shape[0])
        for r0 in range(0, acc.shape[0], rc):
            cos = cos_ref[r0:r0 + rc, :]
            sin = sin_ref[r0:r0 + rc, :]
            for hh in range(acc.shape[1] // LANES):
                a = acc[r0:r0 + rc, hh * LANES:(hh + 1) * LANES]
                rot = pltpu.roll(a, LANES // 2, 1)
                o_ref[r0:r0 + rc, hh * LANES:(hh + 1) * LANES] = (
                    (a * cos + rot * sin) * scale).astype(o_ref.dtype)

    @pl.when(is_q)
    def _():
        store_roped(q_scale)

    @pl.when(is_k)
    def _():
        store_roped(1.0)

    @pl.when(jnp.logical_not(jnp.logical_or(is_q, is_k)))
    def _():
        o_ref[...] = acc.astype(o_ref.dtype)


def _gate_spec(bm, bn, seq, i_gate):
    return pl.BlockSpec((None, None, 1, bn),
                        lambda j, i, te, nt: (jnp.minimum(i, nt[0] - 1) * bm // seq, i_gate, 0, j))


def _tile_spec(bm, bn):
    return pl.BlockSpec((bm, bn), lambda j, i, te, nt: (jnp.minimum(i, nt[0] - 1), j))


def _conv_kernel(val_ref, gate_ref, hval_ref, hgate_ref, w_ref, b_ref, lg_ref, lb_ref,
                 o_ref, u_scr, v_scr, *, ts, seq, rows):
    i = pl.program_id(0)
    first = (i * ts) % seq == 0
    hu = hval_ref[...].astype(f32) * _sigmoid(hgate_ref[...].astype(f32))
    u_scr[0:CONV_HALO, :] = jnp.where(first, 0.0, hu)
    u_scr[CONV_HALO:, :] = val_ref[...].astype(f32) * _sigmoid(gate_ref[...].astype(f32))
    lead = CONV_HALO - (CONV_WIDTH - 1)

    cw = LANES
    sub = 8
    wrows = rows + CONV_HALO

    def chunk(c, carry):
        r0 = pl.multiple_of(c * rows, rows)
        for c0 in range(0, u_scr.shape[1], cw):
            win = u_scr[pl.ds(r0, wrows), c0:c0 + cw]
            acc = jnp.zeros((rows, cw), f32)
            for s in range(sub):
                taps = [j for j in range(CONV_WIDTH) if (lead + j) % sub == s]
                if not taps:
                    continue
                shifted = win if s == 0 else pltpu.roll(win, wrows - s, 0)
                for j in taps:
                    a = (lead + j - s)
                    acc = acc + w_ref[j:j + 1, c0:c0 + cw] * shifted[a:a + rows, :]
            v_scr[pl.ds(r0, rows), c0:c0 + cw] = acc
        return carry

    lax.fori_loop(0, ts // rows, chunk, 0)
    u = v_scr[...] + b_ref[...]
    mu = jnp.mean(u, axis=-1, keepdims=True)
    var = jnp.mean(jnp.square(u - mu), axis=-1, keepdims=True)
    y = (u - mu) * lax.rsqrt(var + NORM_EPS) * lg_ref[...] + lb_ref[...]
    o_ref[...] = _silu(y).astype(o_ref.dtype)


def _conv(proj, conv_w, conv_b, ln_g, ln_b, seq):
    n = proj.shape[0]
    ch = conv_w.shape[1]
    ts = min(TILE["conv_rows"], seq)
    hb = ts // CONV_HALO
    halo_row = lambda i: jnp.maximum(i * hb - 1, 0)
    vec = lambda: pl.BlockSpec((1, ch), lambda i: (0, 0))
    kern = functools.partial(_conv_kernel, ts=ts, seq=seq, rows=min(TILE["conv_chunk"], ts))
    return pl.pallas_call(
        kern,
        grid=(n // ts,),
        in_specs=[
            pl.BlockSpec((ts, ch), lambda i: (i, 0)),
            pl.BlockSpec((ts, ch), lambda i: (i, 1)),
            pl.BlockSpec((CONV_HALO, ch), lambda i: (halo_row(i), 0)),
            pl.BlockSpec((CONV_HALO, ch), lambda i: (halo_row(i), 1)),
            pl.BlockSpec((CONV_WIDTH, ch), lambda i: (0, 0)),
            vec(), vec(), vec(),
        ],
        out_specs=pl.BlockSpec((ts, ch), lambda i: (i, 0)),
        out_shape=jax.ShapeDtypeStruct((n, ch), bf16),
        scratch_shapes=[pltpu.VMEM((ts + CONV_HALO, ch), f32), pltpu.VMEM((ts, ch), f32)],
        compiler_params=_cparams(1),
        name="conformer_conv",
    )(proj, proj, proj, proj, conv_w, conv_b.reshape(1, ch), ln_g.reshape(1, ch), ln_b.reshape(1, ch))


def _attn_kernel(qt_ref, kt_ref, q_ref, k_ref, v_ref, lam_ref, g_ref, o_ref, m_scr, l_scr, acc_scr,
                 *, lambda_init, hd):
    t = pl.program_id(2)
    qi = qt_ref[t]
    ki = kt_ref[t]
    tq = q_ref.shape[0]
    tk = k_ref.shape[0]
    last = ki == (qi + 1) * (tq // tk) - 1
    crosses = (ki + 1) * tk - 1 > qi * tq

    @pl.when(ki == 0)
    def _():
        m_scr[...] = jnp.full(m_scr.shape, -jnp.inf, f32)
        l_scr[...] = jnp.zeros(l_scr.shape, f32)
        acc_scr[...] = jnp.zeros(acc_scr.shape, f32)

    def lane_fold(x, op):
        out = x[:, 0:LANES]
        for cb in range(1, x.shape[1] // LANES):
            out = op(out, x[:, cb * LANES:(cb + 1) * LANES])
        return out

    def step(masked):
        v = v_ref[...]
        for s in range(2):
            q = q_ref[:, s * hd:(s + 1) * hd]
            k = k_ref[:, s * hd:(s + 1) * hd]
            sc = lax.dot_general(q, k, (((1,), (1,)), ((), ())), preferred_element_type=f32)
            if masked:
                row = lax.broadcasted_iota(jnp.int32, sc.shape, 0)
                col = lax.broadcasted_iota(jnp.int32, sc.shape, 1)
                sc = jnp.where(col + (ki * tk - qi * tq) <= row, sc, -jnp.inf)
            m_cur = jnp.max(lane_fold(sc, jnp.maximum), axis=-1, keepdims=True)
            m_prev = m_scr[s]
            m_new = jnp.maximum(m_prev, m_cur)
            alpha = jnp.exp2(m_prev - m_new)
            p = jnp.exp2(sc - jnp.tile(m_new, (1, sc.shape[1] // LANES)))
            l_scr[s] = alpha * l_scr[s] + lane_fold(p, jnp.add)
            acc_scr[s] = (jnp.tile(alpha, (1, v.shape[1] // LANES)) * acc_scr[s]
                          + jnp.dot(p.astype(bf16), v, preferred_element_type=f32))
            m_scr[s] = m_new

    @pl.when(jnp.logical_not(crosses))
    def _():
        step(False)

    @pl.when(crosses)
    def _():
        step(True)

    @pl.when(last)
    def _():
        lp = lam_ref[...]
        lam = (jnp.exp(jnp.sum(lp[0:1] * lp[1:2], axis=-1, keepdims=True))
               - jnp.exp(jnp.sum(lp[2:3] * lp[3:4], axis=-1, keepdims=True)) + lambda_init)
        l0 = jnp.sum(l_scr[0], axis=-1, keepdims=True)
        l1 = jnp.sum(l_scr[1], axis=-1, keepdims=True)
        o = acc_scr[0] / l0 - lam * (acc_scr[1] / l1)
        o = o * lax.rsqrt(jnp.mean(o * o, axis=-1, keepdims=True) + DIFF_SUBLN_EPS)
        o_ref[...] = ((o * g_ref[...]) * (1.0 - lambda_init)).astype(o_ref.dtype)


def _diff_attention(proj, lam_p, subln_g, bsz, seq, q_col0, k_col0, v_col0, lambda_init):
    n = proj.shape[0]
    hd = lam_p.shape[1]
    vd = 2 * hd
    tq = min(TILE["attn_q"], seq)
    tk = min(TILE["attn_kv"], seq)
    nq = seq // tq
    nk = seq // tk
    pairs = [(qi, ki) for qi in range(nq) for ki in range((qi + 1) * (tq // tk))]
    q_of = jnp.asarray([p[0] for p in pairs], jnp.int32)
    k_of = jnp.asarray([p[1] for p in pairs], jnp.int32)
    kern = functools.partial(_attn_kernel, lambda_init=lambda_init, hd=hd)
    return pl.pallas_call(
        kern,
        grid_spec=pltpu.PrefetchScalarGridSpec(
            num_scalar_prefetch=2,
            grid=(bsz, DIFF_HEADS, len(pairs)),
            in_specs=[
                pl.BlockSpec((tq, vd), lambda b, h, t, qt, kt: (b * nq + qt[t], q_col0 // vd + h)),
                pl.BlockSpec((tk, vd), lambda b, h, t, qt, kt: (b * nk + kt[t], k_col0 // vd + h)),
                pl.BlockSpec((tk, vd), lambda b, h, t, qt, kt: (b * nk + kt[t], v_col0 // vd + h)),
                pl.BlockSpec((4, hd), lambda b, h, t, qt, kt: (0, 0)),
                pl.BlockSpec((1, vd), lambda b, h, t, qt, kt: (0, 0)),
            ],
            out_specs=pl.BlockSpec((tq, vd), lambda b, h, t, qt, kt: (b * nq + qt[t], h)),
            scratch_shapes=[pltpu.VMEM((2, tq, LANES), f32), pltpu.VMEM((2, tq, LANES), f32),
                            pltpu.VMEM((2, tq, vd), f32)],
        ),
        out_shape=jax.ShapeDtypeStruct((n, DIFF_HEADS * vd), bf16),
        compiler_params=_cparams(3),
        name="diff_attention",
    )(q_of, k_of, proj, proj, proj, lam_p, subln_g.reshape(1, vd))


def _loga_kernel(h_ref, w1_ref, w2_ref, b_ref, o_ref):
    rowi = lax.broadcasted_iota(jnp.int32, w1_ref.shape, 0)
    w1 = jnp.where(rowi < GLA_RANK, w1_ref[...], 0.0).astype(bf16)
    g1 = lax.dot_general(h_ref[...], w1, (((1,), (1,)), ((), ())), preferred_element_type=f32)
    gpre = jnp.dot(g1.astype(bf16), w2_ref[...].astype(bf16), preferred_element_type=f32) + b_ref[...]
    nx = -gpre
    softplus = jnp.maximum(nx, 0.0) + jnp.log1p(jnp.exp(-jnp.abs(nx)))
    o_ref[...] = -softplus / GLA_TAU


def _log_decay(h, w_in_t, col0, w2, bias):
    n, d = h.shape
    kdim = w2.shape[1]
    bm = min(TILE["logdecay_rows"], n)
    w2p = jnp.zeros((LANES, kdim), f32).at[:GLA_RANK].set(w2)
    return pl.pallas_call(
        _loga_kernel,
        grid=(n // bm,),
        in_specs=[
            pl.BlockSpec((bm, d), lambda i: (i, 0)),
            pl.BlockSpec((None, LANES, d), lambda i: (0, col0 // LANES, 0)),
            pl.BlockSpec((LANES, kdim), lambda i: (0, 0)),
            pl.BlockSpec((1, kdim), lambda i: (0, 0)),
        ],
        out_specs=pl.BlockSpec((bm, kdim), lambda i: (i, 0)),
        out_shape=jax.ShapeDtypeStruct((n, kdim), f32),
        compiler_params=_cparams(1),
        name="gla_log_decay",
    )(h, w_in_t, w2p, bias.reshape(1, kdim))


def _gla_kernel(q_ref, k_ref, v_ref, la_ref, r_ref, g_ref, o_ref, st_ref, *, scale):
    c = pl.program_id(2)

    @pl.when(c == 0)
    def _():
        st_ref[...] = jnp.zeros(st_ref.shape, f32)

    ch = q_ref.shape[0]
    dv = v_ref.shape[1]
    la = la_ref[...]
    row = lax.broadcasted_iota(jnp.int32, (ch, ch), 0)
    col = lax.broadcasted_iota(jnp.int32, (ch, ch), 1)
    causal = row >= col
    dk = la.shape[1]
    la_hi = la.astype(bf16)
    la_lo = (la - la_hi.astype(f32)).astype(bf16)
    la2 = jnp.concatenate([la_hi, la_lo], axis=1)
    b2 = jnp.dot(causal.astype(bf16), la2, preferred_element_type=f32)
    b = b2[:, :dk] + b2[:, dk:]
    b_last = b[ch - 1:ch, :]
    b_mid = b[ch // 2 - 1:ch // 2, :]
    s2 = lax.dot_general(la2, jnp.ones((ch, LANES), bf16), (((0,), (0,)), ((), ())),
                         preferred_element_type=f32)
    b_last_col = s2[:dk] + s2[dk:]
    q = q_ref[...].astype(f32) * scale
    k = k_ref[...].astype(f32)
    q_t = (q * jnp.exp(b)).astype(bf16)
    q_rel = (q * jnp.exp(b - b_mid)).astype(bf16)
    k_rel = (k * jnp.exp(b_mid - b)).astype(bf16)
    k_dec = (k * jnp.exp(b_last - b)).astype(bf16)
    v = v_ref[...]
    attn = lax.dot_general(q_rel, k_rel, (((1,), (1,)), ((), ())), preferred_element_type=f32)
    attn = jnp.where(causal, attn, 0.0)
    st = st_ref[...]
    o = (jnp.dot(attn.astype(bf16), v, preferred_element_type=f32)
         + jnp.dot(q_t, st.astype(bf16), preferred_element_type=f32))
    kv = lax.dot_general(k_dec, v, (((0,), (0,)), ((), ())), preferred_element_type=f32)
    st_ref[...] = jnp.tile(jnp.exp(b_last_col), (1, dv // LANES)) * st + kv
    o = o * lax.rsqrt(jnp.mean(o * o, axis=-1, keepdims=True) + NORM_EPS) * g_ref[...]
    o_ref[...] = (o * _silu(r_ref[...].astype(f32))).astype(o_ref.dtype)


def _gla(proj, log_a, norm_g, bsz, seq, dk, dv):
    n = proj.shape[0]
    ch = 2 * GLA_CHUNK if seq % (2 * GLA_CHUNK) == 0 else GLA_CHUNK
    nc = seq // ch
    kcols = GLA_HEADS * dk
    vcols = GLA_HEADS * dv
    rowb = lambda b, h, c: b * nc + c
    kern = functools.partial(_gla_kernel, scale=dk ** -0.5)
    return pl.pallas_call(
        kern,
        grid=(bsz, GLA_HEADS, nc),
        in_specs=[
            pl.BlockSpec((ch, dk), lambda b, h, c: (rowb(b, h, c), h)),
            pl.BlockSpec((ch, dk), lambda b, h, c: (rowb(b, h, c), kcols // dk + h)),
            pl.BlockSpec((ch, dv), lambda b, h, c: (rowb(b, h, c), 2 * kcols // dv + h)),
            pl.BlockSpec((ch, dk), lambda b, h, c: (rowb(b, h, c), h)),
            pl.BlockSpec((ch, dv), lambda b, h, c: (rowb(b, h, c), (2 * kcols + vcols) // dv + h)),
            pl.BlockSpec((1, dv), lambda b, h, c: (0, 0)),
        ],
        out_specs=pl.BlockSpec((ch, dv), lambda b, h, c: (rowb(b, h, c), h)),
        out_shape=jax.ShapeDtypeStruct((n, vcols), bf16),
        scratch_shapes=[pltpu.VMEM((dk, dv), f32)],
        compiler_params=_cparams(3),
        name="gla_chunked",
    )(proj, proj, proj, log_a, proj, norm_g.reshape(1, dv))


GATHER_UNROLL = 8


def _slab_copy(src_hbm, dst_vmem, sem, src_row, dst_row):
    return pltpu.make_async_copy(src_hbm.at[pl.ds(src_row, 1)], dst_vmem.at[pl.ds(dst_row, 1)], sem)


def _issue_slabs(idx_ref, n_idx, src_hbm, dst, sem):
    def body(r2, carry):
        for u in range(2):
            r = 2 * r2 + u
            _slab_copy(src_hbm, dst, sem, idx_ref[0, r], r).start(priority=u)
        return carry

    lax.fori_loop(0, n_idx // 2, body, 0, unroll=GATHER_UNROLL // 2)


def _wait_slabs(n_idx, src_hbm, dst, sem):
    def body(r, carry):
        _slab_copy(src_hbm, dst, sem, 0, r).wait()
        return carry

    lax.fori_loop(0, n_idx, body, 0, unroll=GATHER_UNROLL)


def _fetch_slabs(cur_ref, nxt_ref, src_hbm, buf, sem, n_idx):
    s = pl.program_id(0)
    slot = s % 2

    @pl.when(s == 0)
    def _():
        _issue_slabs(cur_ref, n_idx, src_hbm, buf.at[0], sem.at[0])

    @pl.when(s + 1 < pl.num_programs(0))
    def _():
        _issue_slabs(nxt_ref, n_idx, src_hbm, buf.at[1 - slot], sem.at[1 - slot])

    _wait_slabs(n_idx, src_hbm, buf.at[slot], sem.at[slot])
    return slot


def _idx_specs(n_steps, n_idx):
    return [
        pl.BlockSpec((None, 1, n_idx), lambda i: (i, 0, 0), memory_space=pltpu.SMEM),
        pl.BlockSpec((None, 1, n_idx), lambda i: (jnp.minimum(i + 1, n_steps - 1), 0, 0),
                     memory_space=pltpu.SMEM),
    ]


def _gather_kernel(cur_ref, nxt_ref, h_hbm, o_ref, buf, sem, *, tr):
    slot = _fetch_slabs(cur_ref, nxt_ref, h_hbm, buf, sem, tr)
    o_ref[...] = _unpack_bf16_pairs(buf[slot].reshape(tr, o_ref.shape[1] // 2))


def _gather_rows(h, row_token, tr):
    n, nslab, _ = h.shape
    r = row_token.shape[0]
    kern = functools.partial(_gather_kernel, tr=tr)
    idx = row_token.reshape(r // tr, 1, tr)
    return pl.pallas_call(
        kern,
        grid=(r // tr,),
        in_specs=_idx_specs(r // tr, tr) + [pl.BlockSpec(memory_space=pl.ANY)],
        out_specs=pl.BlockSpec((tr, 2 * nslab * LANES), lambda i: (i, 0)),
        out_shape=jax.ShapeDtypeStruct((r, 2 * nslab * LANES), bf16),
        scratch_shapes=[pltpu.VMEM((2, tr, nslab, LANES), jnp.uint32), pltpu.SemaphoreType.DMA((2,))],
        compiler_params=_cparams(1),
        name="moe_gather",
    )(idx, idx, h)


def _combine_kernel(cur_ref, nxt_ref, y_hbm, w_ref, x_ref, gate_ref, fn_ref, o_ref, buf, sem, *, tt):
    slot = _fetch_slabs(cur_ref, nxt_ref, y_hbm, buf, sem, TOP_K * tt)
    ys = buf[slot].reshape(TOP_K * tt, x_ref.shape[1])
    w = w_ref[...]
    moe = w[:, 0:1] * ys[0:tt] + w[:, 1:2] * ys[tt:2 * tt]
    x = x_ref[...] + gate_ref[...] * moe
    y = x * lax.rsqrt(jnp.mean(x * x, axis=-1, keepdims=True) + NORM_EPS)
    o_ref[...] = y * fn_ref[...]


def _combine(y, pos, top_w, x, mod, i_gate, final_norm, seq):
    n, d = x.shape
    nslab = d // LANES
    tt = min(TILE["combine_rows"], seq)
    kern = functools.partial(_combine_kernel, tt=tt)
    idx = pos.reshape(n // tt, tt, TOP_K).transpose(0, 2, 1).reshape(n // tt, 1, TOP_K * tt)
    return pl.pallas_call(
        kern,
        grid=(n // tt,),
        in_specs=_idx_specs(n // tt, TOP_K * tt) + [
            pl.BlockSpec(memory_space=pl.ANY),
            pl.BlockSpec((tt, LANES), lambda i: (i, 0)),
            pl.BlockSpec((tt, d), lambda i: (i, 0)),
            pl.BlockSpec((None, None, 1, d), lambda i: (i * tt // seq, i_gate, 0, 0)),
            pl.BlockSpec((1, d), lambda i: (0, 0)),
        ],
        out_specs=pl.BlockSpec((tt, d), lambda i: (i, 0)),
        out_shape=jax.ShapeDtypeStruct((n, d), f32),
        scratch_shapes=[pltpu.VMEM((2, TOP_K * tt, nslab, LANES), f32),
                        pltpu.SemaphoreType.DMA((2,))],
        compiler_params=_cparams(1),
        name="moe_combine_final_norm",
    )(idx, idx, y, top_w, x, mod, final_norm.reshape(1, d))


def _routing_tables(top_i, bm):
    n = top_i.shape[0]
    e_flat = top_i.reshape(-1)
    onehot = (e_flat[:, None] == jnp.arange(N_EXPERTS)[None, :]).astype(jnp.int32)
    counts = jnp.sum(onehot, axis=0)
    rank = jnp.take_along_axis(jnp.cumsum(onehot, axis=0) - onehot, e_flat[:, None], axis=1)[:, 0]
    padded = ((counts + bm - 1) // bm) * bm
    ends = jnp.cumsum(padded)
    starts = ends - padded
    pos = starts[e_flat] + rank
    n_rows = n * TOP_K + N_EXPERTS * bm
    n_tiles = n_rows // bm
    num_tiles = (ends[-1] // bm).astype(jnp.int32)
    tile_start = jnp.minimum(jnp.arange(n_tiles), num_tiles - 1) * bm
    tile_group = jnp.minimum(jnp.searchsorted(ends, tile_start, side="right"),
                             N_EXPERTS - 1).astype(jnp.int32)
    row_token = jnp.zeros((n_rows,), jnp.int32).at[pos].set(jnp.arange(n * TOP_K, dtype=jnp.int32) // TOP_K)
    return pos.astype(jnp.int32), row_token, tile_group, num_tiles.reshape(1)


def _rope_tables(seq, hd):
    inv = ROPE_THETA ** (-jnp.arange(0, hd, 2, dtype=f32) / hd)
    ang = jnp.arange(seq).astype(f32)[:, None] * inv[None, :]
    cos, sin = jnp.cos(ang), jnp.sin(ang)
    return jnp.concatenate([cos, cos], axis=-1), jnp.concatenate([-sin, sin], axis=-1)


def kernel(x, c, norm_gains, ada_w, ada_b, e_w_in, e_conv_w, e_conv_b, e_conv_ln_g, e_conv_ln_b, e_diff_lambda, e_diff_subln, e_w_out, e_ffn_gate, e_ffn_up, e_ffn_down, o_w_in, o_gate_w2, o_gate_b, o_gla_norm, o_w_out, o_router, o_exp_gate, o_exp_up, o_exp_down, final_norm):
    bsz, seq, d = x.shape
    n = bsz * seq
    depth = ada_w.shape[0]
    assert depth == 2, "trunk is one even (conv + diff-attn) and one odd (GLA + experts) layer"
    xf = x.reshape(n, d)
    mods = _ada(c, ada_w, ada_b)
    bm = min(TILE["rows"], seq)
    bn = TILE["cols"]

    mod = mods[0]
    conv_ch = e_conv_w.shape[2]
    hd = e_diff_lambda.shape[2]
    qk_cols = 2 * DIFF_HEADS * hd
    q_col0 = 2 * conv_ch
    k_col0 = q_col0 + qk_cols
    v_col0 = k_col0 + qk_cols
    even_in = e_w_in.shape[2]
    h = _normmod(xf, norm_gains[0, 0], mod, 0, 1, seq)
    cos_t, sin_t = _rope_tables(seq, hd)
    bmi = bm
    bni = max(t for t in (TILE["in_cols"], TILE["in_cols"] // 2, TILE["in_cols"] // 4)
              if all(cc % t == 0 for cc in (q_col0, k_col0, v_col0, even_in)))
    rope_spec = pl.BlockSpec(
        (bmi, hd), lambda j, i, te, nt: (jnp.minimum(i, nt[0] - 1) % (seq // bmi), 0))
    ep = functools.partial(_ep_rope, q_tiles=(q_col0 // bni, k_col0 // bni),
                           k_tiles=(k_col0 // bni, v_col0 // bni),
                           q_scale=hd ** -0.5 * math.log2(math.e))
    proj = _gmm([h], [e_w_in], bm=bmi, bn=bni, n_cols=even_in, epilogue=ep,
                extras=(cos_t, sin_t), extra_specs=(rope_spec, rope_spec),
                out_dtype=bf16, name="even_in_proj")
    y_a = _conv(proj, e_conv_w[0], e_conv_b[0], e_conv_ln_g[0], e_conv_ln_b[0], seq)
    lambda_init = 0.8 - 0.6 * math.exp(-0.3 * 0)
    y_b = _diff_attention(proj, e_diff_lambda[0], e_diff_subln[0], bsz, seq,
                          q_col0, k_col0, v_col0, lambda_init)
    xf = _gmm([y_a, y_b], [e_w_out], bm=bm, bn=bn, n_cols=d, epilogue=_ep_resid,
              extras=(xf, mod), extra_specs=(_tile_spec(bm, bn), _gate_spec(bm, bn, seq, 2)),
              out_dtype=f32, name="even_out_proj")
    h = _normmod(xf, norm_gains[0, 1], mod, 3, 4, seq)
    d_ff = e_ffn_gate.shape[2]
    act = _gmm([h], [e_ffn_gate, e_ffn_up], bm=min(TILE["ffn_rows"], seq), bn=TILE["ffn_cols"],
               n_cols=d_ff, epilogue=_ep_swiglu, out_dtype=bf16, name="ffn_gate_up")
    half = d_ff // 2
    bmd = bm
    part = _gmm([act], [e_ffn_down], bm=bmd, bn=bn, n_cols=d, k_block=half, k_index=0,
                epilogue=_ep_plain, out_dtype=f32, name="ffn_down_lo")
    xf = _gmm([act], [e_ffn_down], bm=bmd, bn=bn, n_cols=d, k_block=half, k_index=1,
              epilogue=_ep_resid_partial,
              extras=(xf, mod, part),
              extra_specs=(_tile_spec(bmd, bn), _gate_spec(bmd, bn, seq, 5), _tile_spec(bmd, bn)),
              out_dtype=f32, name="ffn_down_hi")

    mod = mods[1]
    dk = o_gate_w2.shape[2] // GLA_HEADS
    dv = o_gla_norm.shape[1]
    kcols = GLA_HEADS * dk
    vcols = GLA_HEADS * dv
    main_cols = 2 * kcols + 2 * vcols
    h = _normmod(xf, norm_gains[1, 0], mod, 0, 1, seq)
    w_in_t = jnp.swapaxes(o_w_in, 1, 2)
    proj = _gmm([h], [w_in_t], bm=bm, bn=TILE["in_cols"], n_cols=main_cols, epilogue=_ep_plain,
                out_dtype=bf16, w_t=True, name="odd_in_proj")
    log_a = _log_decay(h, w_in_t, main_cols, o_gate_w2[0], o_gate_b[0])
    o = _gla(proj, log_a, o_gla_norm[0], bsz, seq, dk, dv)
    xf = _gmm([o], [o_w_out], bm=bm, bn=bn, n_cols=d, epilogue=_ep_resid,
              extras=(xf, mod), extra_specs=(_tile_spec(bm, bn), _gate_spec(bm, bn, seq, 2)),
              out_dtype=f32, name="odd_out_proj")
    h32, top_i, top_w = _normmod_router(xf, norm_gains[1, 1], mod, 3, 4, seq, o_router[0])
    bme = min(TILE["expert_rows"], seq)
    pos, row_token, tile_group, num_tiles = _routing_tables(top_i, bme)
    hs = _gather_rows(h32, row_token, min(TILE["gather_rows"], seq))
    d_fe = o_exp_gate.shape[3]
    act = _gmm([hs], [o_exp_gate[0], o_exp_up[0]], bm=bme, bn=min(TILE["expert_up_cols"], d_fe),
               n_cols=d_fe, epilogue=_ep_swiglu, out_dtype=bf16, tile_group=tile_group,
               num_tiles=num_tiles, name="expert_gate_up")
    ye = _gmm([act], [o_exp_down[0]], bm=bme, bn=min(TILE["expert_down_cols"], d), n_cols=d,
              epilogue=_ep_slabs, out_dtype=f32, tile_group=tile_group, num_tiles=num_tiles,
              slab_out=True, name="expert_down")
    out = _combine(ye, pos, top_w, xf, mod, 5, final_norm, seq)
    return out.reshape(bsz, seq, d)
```
